```python
import numpy as np
import jax, jax.numpy as jnp
from jax import lax

D_MODEL = 2048
BATCH = 1
SEQ = 16384
DEPTH = 4

EPS = 1e-6
M_WIDTH = D_MODEL // 4
M_HEAD_DIM = 128
M_HEADS = M_WIDTH // M_HEAD_DIM
M_CHUNK = 128
M_CONV = 4
S_WIDTH = D_MODEL // 4
S_GROUP = 16
S_GROUPS = S_WIDTH // S_GROUP
S_STATE = 64
S_DT_MIN = 0.001
S_DT_MAX = 0.1
N_WIDTH = D_MODEL - M_WIDTH - S_WIDTH
N_HEAD_DIM = 128
N_HEADS = N_WIDTH // N_HEAD_DIM
N_KV_HEADS = N_HEADS // 4
CMP_BLOCK = 32
CMP_STRIDE = 16
CMP_HIDDEN = N_HEAD_DIM
SLC_BLOCK = 64
SLC_TOPK = 16
WINDOW = 512
Q_BLOCK = 128
ROPE_THETA = 10000.0
NEG_INF = -1e30
FORCED_SCORE = 1e9
D_FF = ((8 * D_MODEL // 3 + 127) // 128) * 128
FFN_CONV = 3
PROJ_SIZES = (2 * M_WIDTH, M_WIDTH, M_WIDTH, M_HEADS, M_HEADS, S_WIDTH, N_WIDTH, 6 * N_KV_HEADS * N_HEAD_DIM, 3 * N_HEADS)
D_PROJ = 4 * M_WIDTH + 2 * M_HEADS + S_WIDTH + N_WIDTH + 6 * N_KV_HEADS * N_HEAD_DIM + 3 * N_HEADS

kernel_name = "hybrid_mlstm_s5_nsa_trunk"


def rmsnorm(x, g):
    xf = x.astype(jnp.float32)
    r = lax.rsqrt(jnp.mean(xf * xf, axis=-1, keepdims=True) + EPS)
    return (xf * r).astype(x.dtype) * g


def rope(x, pos):
    half = x.shape[-1] // 2
    inv = ROPE_THETA ** (-jnp.arange(half, dtype=jnp.float32) / half)
    ang = pos.astype(jnp.float32)[..., None] * inv
    cos = jnp.cos(ang)[:, :, None, :]
    sin = jnp.sin(ang)[:, :, None, :]
    x1, x2 = x[..., :half], x[..., half:]
    return jnp.concatenate([x1 * cos - x2 * sin, x1 * sin + x2 * cos], axis=-1)


def causal_dwconv(x, w, b):
    width, ch = w.shape
    y = lax.conv_general_dilated(x, w[:, None, :].astype(x.dtype), window_strides=(1,),
                                 padding=[(width - 1, 0)], dimension_numbers=('NWC', 'WIO', 'NWC'),
                                 feature_group_count=ch)
    return y + b


def mlstm_group(qk, v, o_pre, i_pre, f_pre, conv_w, conv_b, i_bias, f_bias, norm_g):
    f32 = jnp.float32
    B, S, _ = v.shape
    H, Dh, L = M_HEADS, M_HEAD_DIM, M_CHUNK
    NC = S // L
    qk = jax.nn.silu(causal_dwconv(qk.astype(f32), conv_w, conv_b))
    q, k = jnp.split(qk, 2, axis=-1)

    def chunk(a):
        a = a.reshape((B, NC, L, H) + a.shape[3:])
        return jnp.moveaxis(a, 3, 1)

    qc = chunk(q.reshape(B, S, H, Dh))
    kc = chunk(k.reshape(B, S, H, Dh)) * (Dh ** -0.5)
    vc = chunk(v.astype(f32).reshape(B, S, H, Dh))
    ic = chunk(i_pre.astype(f32) + i_bias)
    logf = jax.nn.log_sigmoid(chunk(f_pre.astype(f32) + f_bias))
    g = jnp.cumsum(logf, axis=-1)
    g_last = g[..., -1]
    a = g_last[..., None] - g + ic
    a_max = jnp.max(a, axis=-1)
    w = jnp.exp(a - a_max[..., None])
    C_loc = jnp.einsum('bhcl,bhcld,bhcle->bhcde', w, kc, vc)
    n_loc = jnp.einsum('bhcl,bhcld->bhcd', w, kc)

    def step(carry, xs):
        C, n, m = carry
        Cl, nl, gl, am = xs
        m_new = jnp.maximum(gl + m, am)
        s_old = jnp.exp(gl + m - m_new)
        s_new = jnp.exp(am - m_new)
        C2 = s_old[..., None, None] * C + s_new[..., None, None] * Cl
        n2 = s_old[..., None] * n + s_new[..., None] * nl
        return (C2, n2, m_new), (C, n, m)

    init = (jnp.zeros((B, H, Dh, Dh), f32), jnp.zeros((B, H, Dh), f32), jnp.zeros((B, H), f32))
    xs = (jnp.moveaxis(C_loc, 2, 0), jnp.moveaxis(n_loc, 2, 0), jnp.moveaxis(g_last, 2, 0), jnp.moveaxis(a_max, 2, 0))
    _, (C_prev, n_prev, m_prev) = lax.scan(step, init, xs)
    C_prev = jnp.moveaxis(C_prev, 0, 2)
    n_prev = jnp.moveaxis(n_prev, 0, 2)
    m_prev = jnp.moveaxis(m_prev, 0, 2)

    causal = jnp.tril(jnp.ones((L, L), dtype=bool))
    Dlog = jnp.where(causal, g[..., :, None] - g[..., None, :] + ic[..., None, :], -jnp.inf)
    m_inter = g + m_prev[..., None]
    m_t = jnp.maximum(m_inter, jnp.max(Dlog, axis=-1))
    Sw = jnp.einsum('bhctd,bhcsd->bhcts', qc, kc) * jnp.exp(Dlog - m_t[..., None])
    inter = jnp.exp(m_inter - m_t)
    num = jnp.einsum('bhcts,bhcse->bhcte', Sw, vc) + inter[..., None] * jnp.einsum('bhctd,bhcde->bhcte', qc, C_prev)
    den = jnp.sum(Sw, axis=-1) + inter * jnp.einsum('bhctd,bhcd->bhct', qc, n_prev)
    h = num / jnp.maximum(jnp.abs(den), jnp.exp(-m_t))[..., None]
    h = jnp.moveaxis(h, 1, 3).reshape(B, S, H, Dh)
    h = rmsnorm(h, norm_g)
    h = jax.nn.sigmoid(o_pre.astype(f32)).reshape(B, S, H, Dh) * h
    return h.reshape(B, S, H * Dh)


def s5_group(u, lam_re, lam_im, log_dt, b_re, b_im, c_re, c_im, d_skip, w_glu, b_glu):
    f32 = jnp.float32
    B, S, W = u.shape
    uf = u.astype(f32)
    ug = uf.reshape(B, S, S_GROUPS, S_GROUP)
    lr = lam_re.astype(f32)
    li = lam_im.astype(f32)
    dt = jnp.exp(log_dt.astype(f32))[:, None]
    mag = jnp.exp(lr * dt)
    ar = mag * jnp.cos(li * dt)
    ai = mag * jnp.sin(li * dt)
    den = lr * lr + li * li
    nr = ar - 1.0
    zr = (nr * lr + ai * li) / den
    zi = (ai * lr - nr * li) / den
    br = b_re.astype(f32)
    bi = b_im.astype(f32)
    bbr = zr[..., None] * br - zi[..., None] * bi
    bbi = zr[..., None] * bi + zi[..., None] * br
    xr = jnp.einsum('bsgc,gnc->bsgn', ug, bbr)
    xi = jnp.einsum('bsgc,gnc->bsgn', ug, bbi)
    Ar = jnp.broadcast_to(ar, xr.shape)
    Ai = jnp.broadcast_to(ai, xr.shape)

    def combine(e1, e2):
        a1r, a1i, b1r, b1i = e1
        a2r, a2i, b2r, b2i = e2
        return (a2r * a1r - a2i * a1i, a2r * a1i + a2i * a1r,
                a2r * b1r - a2i * b1i + b2r, a2r * b1i + a2i * b1r + b2i)

    _, _, hr, hi = lax.associative_scan(combine, (Ar, Ai, xr, xi), axis=1)
    y = jnp.einsum('bsgn,gcn->bsgc', hr, c_re.astype(f32)) - jnp.einsum('bsgn,gcn->bsgc', hi, c_im.astype(f32))
    y = y.reshape(B, S, W) + d_skip * uf
    y = jax.nn.gelu(y)
    return y * jax.nn.sigmoid(y @ w_glu + b_glu)


def compress(x, pos_emb, w1, w2):
    B, S, Hk, Dh = x.shape
    NP = S // CMP_STRIDE
    pieces = x.reshape(B, NP, CMP_STRIDE, Hk, Dh)
    blocks = jnp.concatenate([pieces[:, :-1], pieces[:, 1:]], axis=2)
    blocks = blocks + pos_emb[:, None, :]
    flat = jnp.moveaxis(blocks, 3, 2).reshape(B, NP - 1, Hk, CMP_BLOCK * Dh)
    return jax.nn.gelu(flat @ w1) @ w2


def nsa_group(q, k_cmp, v_cmp, k_slc, v_slc, k_win, v_win, gates, cmp_pos, cmp_w1, cmp_w2):
    B, S, H, Dh = q.shape
    Hk = k_slc.shape[2]
    G = H // Hk
    n_slc = S // SLC_BLOCK
    ratio = SLC_BLOCK // CMP_STRIDE
    top = min(SLC_TOPK, n_slc)
    scale = Dh ** -0.5
    kc = compress(k_cmp, cmp_pos[0], cmp_w1[0], cmp_w2[0])
    vc = compress(v_cmp, cmp_pos[1], cmp_w1[1], cmp_w2[1])
    n_cmp = kc.shape[1]
    pad_cmp = n_slc * ratio - n_cmp
    blk_end = jnp.arange(n_cmp) * CMP_STRIDE + (CMP_BLOCK - 1)
    ks_blocks = jnp.moveaxis(k_slc.reshape(B, n_slc, SLC_BLOCK, Hk, Dh), 3, 1)
    vs_blocks = jnp.moveaxis(v_slc.reshape(B, n_slc, SLC_BLOCK, Hk, Dh), 3, 1)
    kw_pad = jnp.pad(k_win, ((0, 0), (WINDOW, 0), (0, 0), (0, 0)))
    vw_pad = jnp.pad(v_win, ((0, 0), (WINDOW, 0), (0, 0), (0, 0)))
    bidx = jnp.arange(B)[:, None, None, None]
    hidx = jnp.arange(Hk)[None, :, None, None]
    slc_j = jnp.arange(n_slc)
    in_blk = jnp.arange(SLC_BLOCK)
    win_off = jnp.arange(WINDOW + Q_BLOCK) - WINDOW

    def block(qb_idx):
        start = qb_idx * Q_BLOCK
        t = start + jnp.arange(Q_BLOCK)
        qb = lax.dynamic_slice_in_dim(q, start, Q_BLOCK, axis=1).reshape(B, Q_BLOCK, Hk, G, Dh)
        gb = lax.dynamic_slice_in_dim(gates, start, Q_BLOCK, axis=1).reshape(B, Q_BLOCK, Hk, G, 3)
        s = jnp.einsum('btkgd,bjkd->bkgtj', qb, kc) * scale
        valid = blk_end[None, :] <= t[:, None]
        p_cmp = jax.nn.softmax(jnp.where(valid, s, NEG_INF), axis=-1) * valid
        o_cmp = jnp.einsum('bkgtj,bjkd->btkgd', p_cmp, vc)
        pk = jnp.pad(p_cmp.sum(axis=2), ((0, 0), (0, 0), (0, 0), (0, pad_cmp)))
        pk = pk.reshape(B, Hk, Q_BLOCK, n_slc, ratio)
        last = pk[..., ratio - 1]
        imp = jnp.pad(last, ((0, 0), (0, 0), (0, 0), (1, 0)))[..., :-1] + 2.0 * jnp.sum(pk[..., :ratio - 1], axis=-1) + last
        tb = t // SLC_BLOCK
        allowed = slc_j[None, :] <= tb[:, None]
        forced = (slc_j[None, :] == 0) | (slc_j[None, :] == tb[:, None]) | (slc_j[None, :] == tb[:, None] - 1)
        score = jnp.where(forced, FORCED_SCORE, jnp.where(allowed, imp, -jnp.inf))
        _, idx = lax.top_k(score, top)
        ks = ks_blocks[bidx, hidx, idx].reshape(B, Hk, Q_BLOCK, top * SLC_BLOCK, Dh)
        vs = vs_blocks[bidx, hidx, idx].reshape(B, Hk, Q_BLOCK, top * SLC_BLOCK, Dh)
        kpos = (idx[..., None] * SLC_BLOCK + in_blk).reshape(B, Hk, Q_BLOCK, top * SLC_BLOCK)
        s = jnp.einsum('btkgd,bktnd->bkgtn', qb, ks) * scale
        smask = (kpos <= t[:, None])[:, :, None]
        p = jax.nn.softmax(jnp.where(smask, s, NEG_INF), axis=-1)
        o_slc = jnp.einsum('bkgtn,bktnd->btkgd', p, vs)
        kw = lax.dynamic_slice_in_dim(kw_pad, start, WINDOW + Q_BLOCK, axis=1)
        vw = lax.dynamic_slice_in_dim(vw_pad, start, WINDOW + Q_BLOCK, axis=1)
        wpos = start + win_off
        wmask = (wpos[None, :] <= t[:, None]) & (wpos[None, :] > t[:, None] - WINDOW) & (wpos[None, :] >= 0)
        s = jnp.einsum('btkgd,bnkd->bkgtn', qb, kw) * scale
        p = jax.nn.softmax(jnp.where(wmask, s, NEG_INF), axis=-1)
        o_win = jnp.einsum('bkgtn,bnkd->btkgd', p, vw)
        o = gb[..., 0:1] * o_cmp + gb[..., 1:2] * o_slc + gb[..., 2:3] * o_win
        return o.reshape(B, Q_BLOCK, H * Dh)

    out = lax.map(block, jnp.arange(S // Q_BLOCK))
    return jnp.moveaxis(out, 0, 1).reshape(B, S, H * Dh)


def hybrid_mixer(h, positions, w_in, w_out, m_conv_w, m_conv_b, m_i_bias, m_f_bias, m_norm_g,
                 s_lam_re, s_lam_im, s_log_dt, s_b_re, s_b_im, s_c_re, s_c_im, s_d, s_w_glu, s_b_glu,
                 n_cmp_pos, n_cmp_w1, n_cmp_w2):
    f32 = jnp.float32
    B, S, _ = h.shape
    proj = (h @ w_in).astype(f32)
    offs = np.cumsum(PROJ_SIZES)[:-1].tolist()
    m_qk, m_v, m_o, m_i, m_f, s_u, n_q, n_kv, n_g = jnp.split(proj, offs, axis=-1)
    y_m = mlstm_group(m_qk, m_v, m_o, m_i, m_f, m_conv_w, m_conv_b, m_i_bias, m_f_bias, m_norm_g)
    y_s = s5_group(s_u, s_lam_re, s_lam_im, s_log_dt, s_b_re, s_b_im, s_c_re, s_c_im, s_d, s_w_glu, s_b_glu)
    q = rope(n_q.reshape(B, S, N_HEADS, N_HEAD_DIM), positions)
    kv = n_kv.reshape(B, S, 6, N_KV_HEADS, N_HEAD_DIM)
    k_cmp = rope(kv[:, :, 0], positions)
    k_slc = rope(kv[:, :, 2], positions)
    k_win = rope(kv[:, :, 4], positions)
    gates = jax.nn.sigmoid(n_g).reshape(B, S, N_HEADS, 3)
    y_n = nsa_group(q, k_cmp, kv[:, :, 1], k_slc, kv[:, :, 3], k_win, kv[:, :, 5], gates,
                    n_cmp_pos, n_cmp_w1, n_cmp_w2)
    y = jnp.concatenate([y_m, y_s, y_n], axis=-1).astype(h.dtype)
    return y @ w_out


def conv_ffn(h, w_up, conv_w, conv_b, w_down):
    a, u = jnp.split(h @ w_up, 2, axis=-1)
    a = causal_dwconv(a, conv_w, conv_b)
    return (jax.nn.gelu(a) * u) @ w_down


def setup_inputs(seed: int = 0) -> dict:
    key = jax.random.key(seed)
    keys = iter(jax.random.split(key, 40))
    f32 = jnp.float32

    def nrm(shape, s):
        return jax.random.normal(next(keys), shape, f32) * s

    L, D = DEPTH, D_MODEL
    x = nrm((BATCH, SEQ, D), 1.0)
    c = nrm((BATCH, D), 1.0)
    offset = jax.random.randint(next(keys), (BATCH, 1), 0, 1024, dtype=jnp.int32)
    positions = offset + jnp.arange(SEQ, dtype=jnp.int32)[None, :]
    w_ada = nrm((L, D, 6 * D), 0.5 * D ** -0.5)
    b_ada = nrm((L, 6 * D), 0.02)
    g_pre_mix = 1.0 + nrm((L, D), 0.02)
    g_post_mix = 1.0 + nrm((L, D), 0.02)
    g_pre_ffn = 1.0 + nrm((L, D), 0.02)
    g_post_ffn = 1.0 + nrm((L, D), 0.02)
    w_in = nrm((L, D, D_PROJ), D ** -0.5)
    w_out = nrm((L, D, D), D ** -0.5)
    m_conv_w = nrm((L, M_CONV, 2 * M_WIDTH), M_CONV ** -0.5)
    m_conv_b = nrm((L, 2 * M_WIDTH), 0.02)
    m_i_bias = nrm((L, M_HEADS), 0.1)
    m_f_bias = jnp.linspace(3.0, 6.0, M_HEADS, dtype=f32)[None, :] + nrm((L, M_HEADS), 0.1)
    m_norm_g = 1.0 + nrm((L, M_HEADS, M_HEAD_DIM), 0.02)
    s_lam_re = -0.5 + nrm((L, S_GROUPS, S_STATE), 0.01)
    s_lam_im = jnp.pi * jnp.arange(S_STATE, dtype=f32) + nrm((L, S_GROUPS, S_STATE), 0.01)
    s_log_dt = jax.random.uniform(next(keys), (L, S_GROUPS), f32, float(np.log(S_DT_MIN)), float(np.log(S_DT_MAX)))
    s_b_re = nrm((L, S_GROUPS, S_STATE, S_GROUP), (2 * S_GROUP) ** -0.5)
    s_b_im = nrm((L, S_GROUPS, S_STATE, S_GROUP), (2 * S_GROUP) ** -0.5)
    s_c_re = nrm((L, S_GROUPS, S_GROUP, S_STATE), S_STATE ** -0.5)
    s_c_im = nrm((L, S_GROUPS, S_GROUP, S_STATE), S_STATE ** -0.5)
    s_d = nrm((L, S_WIDTH), 0.5)
    s_w_glu = nrm((L, S_WIDTH, S_WIDTH), S_WIDTH ** -0.5)
    s_b_glu = nrm((L, S_WIDTH), 0.02)
    n_cmp_pos = nrm((L, 2, CMP_BLOCK, N_HEAD_DIM), 0.1)
    n_cmp_w1 = nrm((L, 2, CMP_BLOCK * N_HEAD_DIM, CMP_HIDDEN), (CMP_BLOCK * N_HEAD_DIM) ** -0.5)
    n_cmp_w2 = nrm((L, 2, CMP_HIDDEN, N_HEAD_DIM), CMP_HIDDEN ** -0.5)
    f_w_up = nrm((L, D, 2 * D_FF), D ** -0.5)
    f_conv_w = nrm((L, FFN_CONV, D_FF), FFN_CONV ** -0.5)
    f_conv_b = nrm((L, D_FF), 0.02)
    f_w_down = nrm((L, D_FF, D), D_FF ** -0.5)
    return {"x": x, "c": c, "positions": positions, "w_ada": w_ada, "b_ada": b_ada,
            "g_pre_mix": g_pre_mix, "g_post_mix": g_post_mix, "g_pre_ffn": g_pre_ffn, "g_post_ffn": g_post_ffn,
            "w_in": w_in, "w_out": w_out, "m_conv_w": m_conv_w, "m_conv_b": m_conv_b,
            "m_i_bias": m_i_bias, "m_f_bias": m_f_bias, "m_norm_g": m_norm_g,
            "s_lam_re": s_lam_re, "s_lam_im": s_lam_im, "s_log_dt": s_log_dt, "s_b_re": s_b_re, "s_b_im": s_b_im,
            "s_c_re": s_c_re, "s_c_im": s_c_im, "s_d": s_d, "s_w_glu": s_w_glu, "s_b_glu": s_b_glu,
            "n_cmp_pos": n_cmp_pos, "n_cmp_w1": n_cmp_w1, "n_cmp_w2": n_cmp_w2,
            "f_w_up": f_w_up, "f_conv_w": f_conv_w, "f_conv_b": f_conv_b, "f_w_down": f_w_down}


def reference(x, c, positions, w_ada, b_ada, g_pre_mix, g_post_mix, g_pre_ffn, g_post_ffn,
              w_in, w_out, m_conv_w, m_conv_b, m_i_bias, m_f_bias, m_norm_g,
              s_lam_re, s_lam_im, s_log_dt, s_b_re, s_b_im, s_c_re, s_c_im, s_d, s_w_glu, s_b_glu,
              n_cmp_pos, n_cmp_w1, n_cmp_w2, f_w_up, f_conv_w, f_conv_b, f_w_down):
    B, S, D = x.shape
    cs = jax.nn.silu(c)
    for l in range(DEPTH):
        mod = (cs @ w_ada[l] + b_ada[l]).reshape(B, 6, D)
        sh1, sc1, gt1, sh2, sc2, gt2 = [mod[:, i][:, None, :] for i in range(6)]
        h = rmsnorm(x, g_pre_mix[l]) * (1.0 + sc1) + sh1
        y = hybrid_mixer(h, positions, w_in[l], w_out[l], m_conv_w[l], m_conv_b[l], m_i_bias[l], m_f_bias[l],
                         m_norm_g[l], s_lam_re[l], s_lam_im[l], s_log_dt[l], s_b_re[l], s_b_im[l], s_c_re[l],
                         s_c_im[l], s_d[l], s_w_glu[l], s_b_glu[l], n_cmp_pos[l], n_cmp_w1[l], n_cmp_w2[l])
        x = x + gt1 * rmsnorm(y, g_post_mix[l])
        h = rmsnorm(x, g_pre_ffn[l]) * (1.0 + sc2) + sh2
        y = conv_ffn(h, f_w_up[l], f_conv_w[l], f_conv_b[l], f_w_down[l])
        x = x + gt2 * rmsnorm(y, g_post_ffn[l])
    return x
```

```python
import functools

import numpy as np
import jax
import jax.numpy as jnp
from jax import lax
from jax.experimental import pallas as pl
from jax.experimental.pallas import tpu as pltpu

F32 = jnp.float32
BF16 = jnp.bfloat16

D_MODEL = 2048
EPS = 1e-6
M_WIDTH = 512
M_HEADS = 4
M_HEAD_DIM = 128
M_CHUNK = 128
M_CONV = 4
S_WIDTH = 512
S_GROUP = 16
S_GROUPS = 32
S_STATE = 64
S_T = 16
S_PAIRS = S_GROUPS // 2
N_WIDTH = 1024
N_HEAD_DIM = 128
N_HEADS = 8
N_KV_HEADS = 2
N_REP = N_HEADS // N_KV_HEADS
CMP_BLOCK = 32
CMP_STRIDE = 16
SLC_BLOCK = 64
SLC_TOPK = 16
WINDOW = 512
Q_BLOCK = 128
ROPE_THETA = 10000.0
NEG_INF = -1e30
FORCED_SCORE = 1e9
SLC_TILE = 512
D_FF = 5504
D_FF_PAD = 5632
FFN_TILE = 512
COL_MQK, COL_MV, COL_MO, COL_SU, COL_NQ, COL_NKV, COL_SMALL = 0, 1024, 1536, 2048, 2560, 3584, 5120
N_PROJ_PAD = 5376
PROJ_TILE = 768
SMALL_I, SMALL_F, SMALL_G = 0, 4, 8

VMEM_LIMIT_BYTES = 58 * 1024 * 1024


def _params(*sem):
    return pltpu.CompilerParams(dimension_semantics=sem, vmem_limit_bytes=VMEM_LIMIT_BYTES)


def _gelu(x):
    return 0.5 * x * (1.0 + jnp.tanh(0.7978845608028654 * (x + 0.044715 * (x * x * x))))


def _rms(x):
    return lax.rsqrt(jnp.mean(x * x, axis=-1, keepdims=True) + EPS)


def _dot(a, b):
    return jnp.dot(a, b, preferred_element_type=F32)


def _dot_nt(a, b):
    return lax.dot_general(a, b, (((1,), (1,)), ((), ())), preferred_element_type=F32)


def _ada_kernel(c_ref, w_ref, b_ref, o_ref):
    c = c_ref[...]
    cs = c * jax.nn.sigmoid(c)
    o_ref[...] = jnp.sum(w_ref[...] * cs, axis=0, keepdims=True) + b_ref[...]


def _ada_mod(c_col, w_ada, b_ada):
    depth, d, n = w_ada.shape
    tn = 1024
    return pl.pallas_call(
        _ada_kernel,
        grid=(depth, n // tn),
        in_specs=[pl.BlockSpec((d, 1), lambda l, j: (0, 0)),
                  pl.BlockSpec((None, d, tn), lambda l, j: (l, 0, j)),
                  pl.BlockSpec((None, 1, tn), lambda l, j: (l, 0, j))],
        out_specs=pl.BlockSpec((None, 1, tn), lambda l, j: (l, 0, j)),
        out_shape=jax.ShapeDtypeStruct((depth, 1, n), F32),
        compiler_params=_params("arbitrary", "arbitrary"),
    )(c_col, w_ada, b_ada.reshape(depth, 1, n))


def _inproj_kernel(x_ref, g_ref, sc_ref, sh_ref, w_ref, o_ref, h_ref):
    @pl.when(pl.program_id(1) == 0)
    def _():
        x = x_ref[...]
        h = (x * _rms(x)) * g_ref[...]
        h_ref[...] = (h * (1.0 + sc_ref[...]) + sh_ref[...]).astype(BF16)

    o_ref[...] = _dot(h_ref[...], w_ref[...])


def _inproj(x, g, sc, sh, w):
    s, d = x.shape
    n = w.shape[1]
    tm, tn = 512, PROJ_TILE
    row = pl.BlockSpec((1, d), lambda i, j: (0, 0))
    return pl.pallas_call(
        _inproj_kernel,
        grid=(s // tm, n // tn),
        in_specs=[pl.BlockSpec((tm, d), lambda i, j: (i, 0)), row, row, row,
                  pl.BlockSpec((d, tn), lambda i, j: (0, j))],
        out_specs=pl.BlockSpec((tm, tn), lambda i, j: (i, j)),
        out_shape=jax.ShapeDtypeStruct((s, n), F32),
        scratch_shapes=[pltpu.VMEM((tm, d), BF16)],
        compiler_params=_params("arbitrary", "arbitrary"),
    )(x, g, sc, sh, w)


def _mlstm_kernel(qk_ref, v_ref, o_ref, sm_ref, cw_ref, cb_ref, gb_ref, ng_ref, y_ref,
                  tail_ref, c_ref, n_ref, m_ref):
    L, Dh = M_CHUNK, M_HEAD_DIM

    @pl.when(pl.program_id(0) == 0)
    def _():
        tail_ref[...] = jnp.zeros_like(tail_ref)
        c_ref[...] = jnp.zeros_like(c_ref)
        n_ref[...] = jnp.zeros_like(n_ref)
        m_ref[...] = jnp.zeros_like(m_ref)

    raw = qk_ref[...]
    tail = tail_ref[...]
    row = lax.broadcasted_iota(jnp.int32, raw.shape, 0)

    def shifted(d):
        r = pltpu.roll(raw, d, axis=0)
        for k in range(d):
            r = jnp.where(row == k, tail[8 - d + k:8 - d + k + 1, :], r)
        return r

    cw = cw_ref[...]
    pre = (cw[3:4] * raw + cw[2:3] * shifted(1) + cw[1:2] * shifted(2) + cw[0:1] * shifted(3)
           + cb_ref[...])
    tail_ref[...] = raw[L - 8:L, :]
    qk = pre * jax.nn.sigmoid(pre)

    sm = sm_ref[...] + gb_ref[...]
    lf = jnp.minimum(sm, 0.0) - jnp.log1p(jnp.exp(-jnp.abs(sm)))
    rowg = lax.broadcasted_iota(jnp.int32, lf.shape, 0)
    g = lf
    for s in (1, 2, 4, 8, 16, 32, 64):
        g = g + jnp.where(rowg >= s, pltpu.roll(g, s, axis=0), 0.0)
    smT = sm.T
    gT = g.T
    ri = lax.broadcasted_iota(jnp.int32, (L, L), 0)
    ci = lax.broadcasted_iota(jnp.int32, (L, L), 1)
    causal = ri >= ci

    for h in range(M_HEADS):
        hs = slice(h * Dh, (h + 1) * Dh)
        q = qk[:, h * Dh:(h + 1) * Dh]
        k = qk[:, M_WIDTH + h * Dh:M_WIDTH + (h + 1) * Dh] * (Dh ** -0.5)
        v = v_ref[:, hs]
        i_c = sm[:, SMALL_I + h:SMALL_I + h + 1]
        g_c = g[:, SMALL_F + h:SMALL_F + h + 1]
        i_r = smT[SMALL_I + h:SMALL_I + h + 1, :]
        g_r = gT[SMALL_F + h:SMALL_F + h + 1, :]
        g_last = g_r[:, L - 1:L]
        a_r = g_last - g_r + i_r
        a_max = jnp.max(a_r, axis=1, keepdims=True)
        w_c = jnp.exp(g_last - g_c + i_c - a_max)
        m_prev = m_ref[h:h + 1, 0:1]
        c_prev = c_ref[h]
        n_prev = n_ref[h:h + 1, :]
        qb = q.astype(BF16)
        kb = k.astype(BF16)
        vb = v.astype(BF16)
        c_loc = _dot(k.T.astype(BF16), (w_c * v).astype(BF16))
        n_loc = jnp.sum(w_c * k, axis=0, keepdims=True)
        m_new = jnp.maximum(g_last + m_prev, a_max)
        s_old = jnp.exp(g_last + m_prev - m_new)
        s_new = jnp.exp(a_max - m_new)
        c_ref[h] = s_old * c_prev + s_new * c_loc
        n_ref[h:h + 1, :] = s_old * n_prev + s_new * n_loc
        m_ref[h:h + 1, :] = jnp.broadcast_to(m_new, (1, Dh))
        dlog = jnp.where(causal, g_c - g_r + i_r, -jnp.inf)
        m_inter = g_c + m_prev
        m_t = jnp.maximum(m_inter, jnp.max(dlog, axis=1, keepdims=True))
        sw = _dot_nt(qb, kb) * jnp.exp(dlog - m_t)
        inter = jnp.exp(m_inter - m_t)
        num = _dot(sw.astype(BF16), vb) + inter * _dot(qb, c_prev.astype(BF16))
        den = jnp.sum(sw, axis=1, keepdims=True) + inter * jnp.sum(q * n_prev, axis=1, keepdims=True)
        hh = num / jnp.maximum(jnp.abs(den), jnp.exp(-m_t))
        hn = (hh * _rms(hh)) * ng_ref[:, hs]
        y_ref[:, hs] = (jax.nn.sigmoid(o_ref[:, hs]) * hn).astype(BF16)


def _mlstm(proj, conv_w, conv_b, gate_bias, norm_g):
    s = proj.shape[0]
    L = M_CHUNK
    full = lambda shape: pl.BlockSpec(shape, lambda c: (0,) * len(shape))
    return pl.pallas_call(
        _mlstm_kernel,
        grid=(s // L,),
        in_specs=[pl.BlockSpec((L, 2 * M_WIDTH), lambda c: (c, COL_MQK // (2 * M_WIDTH))),
                  pl.BlockSpec((L, M_WIDTH), lambda c: (c, COL_MV // M_WIDTH)),
                  pl.BlockSpec((L, M_WIDTH), lambda c: (c, COL_MO // M_WIDTH)),
                  pl.BlockSpec((L, 128), lambda c: (c, COL_SMALL // 128)),
                  full((M_CONV, 2 * M_WIDTH)), full((1, 2 * M_WIDTH)), full((1, 128)), full((1, M_WIDTH))],
        out_specs=pl.BlockSpec((L, M_WIDTH), lambda c: (c, 0)),
        out_shape=jax.ShapeDtypeStruct((s, M_WIDTH), BF16),
        scratch_shapes=[pltpu.VMEM((8, 2 * M_WIDTH), F32),
                        pltpu.VMEM((M_HEADS, M_HEAD_DIM, M_HEAD_DIM), F32),
                        pltpu.VMEM((8, M_HEAD_DIM), F32),
                        pltpu.VMEM((8, M_HEAD_DIM), F32)],
        compiler_params=_params("arbitrary"),
    )(proj, proj, proj, proj, conv_w, conv_b, gate_bias, norm_g)


def _split3(a):
    hi = a.astype(BF16)
    r = a - hi.astype(F32)
    mid = r.astype(BF16)
    lo = (r - mid.astype(F32)).astype(BF16)
    return hi, mid, lo


def _dot_f32(a, b):
    ah, am, al = _split3(a)
    bh, bm, bl = _split3(b)
    small = _dot(ah, bl) + _dot(am, bm) + _dot(al, bh)
    mid = _dot(ah, bm) + _dot(am, bh)
    return _dot(ah, bh) + (mid + small)


def _s5_prep_kernel(lrr_ref, lir_ref, lrc_ref, lic_ref, ldt_ref, btr_ref, bti_ref, ctr_ref, cti_ref,
                    m_ref, pr_ref, pi_ref, rr_ref, ri_ref, ar_ref, ai_ref):
    T = S_T
    dt = jnp.exp(ldt_ref[...])
    lr, li = lrr_ref[...], lir_ref[...]
    mag = jnp.exp(lr * dt)
    a_re = mag * jnp.cos(li * dt)
    a_im = mag * jnp.sin(li * dt)
    den = lr * lr + li * li
    nr = a_re - 1.0
    z_re = (nr * lr + a_im * li) / den
    z_im = (a_im * lr - nr * li) / den
    btr, bti = btr_ref[...], bti_ref[...]
    bb_re = z_re * btr - z_im * bti
    bb_im = z_re * bti + z_im * btr

    def pow_row(e):
        mg = jnp.exp((lr * dt) * e)
        th = (li * dt) * e
        return mg * jnp.cos(th), mg * jnp.sin(th)

    jrow = lax.shift_right_logical(lax.broadcasted_iota(jnp.int32, btr.shape, 0), 4).astype(F32)
    p_re, p_im = pow_row(-jrow)
    left_re = bb_re * p_re - bb_im * p_im
    left_im = bb_re * p_im + bb_im * p_re
    q_re, q_im = pow_row((T - 1.0) - jrow)
    pr_ref[...] = bb_re * q_re - bb_im * q_im
    pi_ref[...] = bb_re * q_im + bb_im * q_re

    lrc, lic = lrc_ref[...], lic_ref[...]
    ctr, cti = ctr_ref[...], cti_ref[...]

    def pow_col(e):
        mg = jnp.exp((lrc * dt) * e)
        th = (lic * dt) * e
        return mg * jnp.cos(th), mg * jnp.sin(th)

    icol = lax.shift_right_logical(lax.broadcasted_iota(jnp.int32, ctr.shape, 1), 4).astype(F32)
    e_re, e_im = pow_col(icol)
    right_re = ctr * e_re - cti * e_im
    right_im = ctr * e_im + cti * e_re
    m = _dot_f32(left_re, right_re) - _dot_f32(left_im, right_im)
    mj = lax.shift_right_logical(lax.broadcasted_iota(jnp.int32, m.shape, 0), 4)
    mi = lax.shift_right_logical(lax.broadcasted_iota(jnp.int32, m.shape, 1), 4)
    m_ref[...] = jnp.where(mi >= mj, m, 0.0)
    f_re, f_im = pow_col(icol + 1.0)
    rr_ref[...] = ctr * f_re - cti * f_im
    ri_ref[...] = -(ctr * f_im + cti * f_re)
    mag_t = jnp.exp((lr * dt) * T)
    ar_ref[...] = mag_t * jnp.cos((li * dt) * T)
    ai_ref[...] = mag_t * jnp.sin((li * dt) * T)


def _s5_prep(lam_re, lam_im, log_dt, b_re, b_im, c_re, c_im):
    G, N, C, T = S_GROUPS, S_STATE, S_GROUP, S_T
    TC = T * C
    bt = lambda b: jnp.tile(jnp.swapaxes(b, 1, 2), (1, T, 1))
    ct = lambda c: jnp.tile(jnp.swapaxes(c, 1, 2), (1, 1, T))
    blk = lambda a, b: pl.BlockSpec((None, a, b), lambda g: (g, 0, 0))
    outs = pl.pallas_call(
        _s5_prep_kernel,
        grid=(G,),
        in_specs=[blk(1, N), blk(1, N), blk(N, 1), blk(N, 1), blk(1, 1),
                  blk(TC, N), blk(TC, N), blk(N, TC), blk(N, TC)],
        out_specs=[blk(TC, TC), blk(TC, N), blk(TC, N), blk(N, TC), blk(N, TC), blk(1, N), blk(1, N)],
        out_shape=[jax.ShapeDtypeStruct((G, TC, TC), F32),
                   jax.ShapeDtypeStruct((G, TC, N), F32), jax.ShapeDtypeStruct((G, TC, N), F32),
                   jax.ShapeDtypeStruct((G, N, TC), F32), jax.ShapeDtypeStruct((G, N, TC), F32),
                   jax.ShapeDtypeStruct((G, 1, N), F32), jax.ShapeDtypeStruct((G, 1, N), F32)],
        compiler_params=_params("arbitrary"),
    )(lam_re.reshape(G, 1, N), lam_im.reshape(G, 1, N), lam_re.reshape(G, N, 1), lam_im.reshape(G, N, 1),
      log_dt.reshape(G, 1, 1), bt(b_re), bt(b_im), ct(c_re), ct(c_im))
    m, pr, pi, rr, ri, ar, ai = outs

    def blockdiag(a):
        g, r, c = a.shape
        a = a.reshape(g // 2, 2, r, c)
        z = jnp.zeros((g // 2, r, c), a.dtype)
        top = jnp.concatenate([a[:, 0], z], axis=2)
        bot = jnp.concatenate([z, a[:, 1]], axis=2)
        return jnp.concatenate([top, bot], axis=1)

    pair_row = lambda a: a.reshape(S_PAIRS, 1, 2 * N)
    return (blockdiag(m).astype(BF16), blockdiag(pr).astype(BF16), blockdiag(pi).astype(BF16),
            blockdiag(rr).astype(BF16), blockdiag(ri).astype(BF16), pair_row(ar), pair_row(ai))


def _s5_kernel(u_ref, m_ref, pr_ref, pi_ref, rr_ref, ri_ref, ar_ref, ai_ref, y_ref,
               xr_s, xi_s, er_s, ei_s):
    u = u_ref[...]
    nch = u.shape[0]
    y_loc = _dot(u, m_ref[...])
    xr_s[...] = _dot(u, pr_ref[...])
    xi_s[...] = _dot(u, pi_ref[...])
    a_re, a_im = ar_ref[...], ai_ref[...]

    def body(b, carry):
        e_re, e_im = carry
        base = pl.multiple_of(b * 8, 8)
        xr = xr_s[pl.ds(base, 8), :]
        xi = xi_s[pl.ds(base, 8), :]
        for k in range(8):
            er_s[pl.ds(base + k, 1), :] = e_re
            ei_s[pl.ds(base + k, 1), :] = e_im
            n_re = a_re * e_re - a_im * e_im + xr[k:k + 1]
            n_im = a_re * e_im + a_im * e_re + xi[k:k + 1]
            e_re, e_im = n_re, n_im
        return e_re, e_im

    zero = jnp.zeros((1, a_re.shape[1]), F32)
    lax.fori_loop(0, nch // 8, body, (zero, zero))
    y_ref[...] = (y_loc + _dot(er_s[...].astype(BF16), rr_ref[...])
                  + _dot(ei_s[...].astype(BF16), ri_ref[...]))


def _s5_scan(u_pairs, ops):
    m, pr, pi, rr, ri, ar, ai = ops
    npair, nch, w = u_pairs.shape
    ns = ar.shape[2]
    blk = lambda a, b: pl.BlockSpec((None, a, b), lambda p: (p, 0, 0))
    return pl.pallas_call(
        _s5_kernel,
        grid=(npair,),
        in_specs=[blk(nch, w), blk(w, w), blk(w, ns), blk(w, ns), blk(ns, w), blk(ns, w),
                  blk(1, ns), blk(1, ns)],
        out_specs=blk(nch, w),
        out_shape=jax.ShapeDtypeStruct((npair, nch, w), F32),
        scratch_shapes=[pltpu.VMEM((nch, ns), F32) for _ in range(4)],
        compiler_params=_params("arbitrary"),
    )(u_pairs, m, pr, pi, rr, ri, ar, ai)


def _s5_glu_kernel(y_ref, u_ref, d_ref, w_ref, b_ref, o_ref):
    y = _gelu(y_ref[...] + d_ref[...] * u_ref[...])
    z = _dot(y.astype(BF16), w_ref[...]) + b_ref[...]
    o_ref[...] = (y * jax.nn.sigmoid(z)).astype(BF16)


def _s5_glu(y, proj, d_skip, w_glu, b_glu):
    s, w = y.shape
    tm = 1024
    row = pl.BlockSpec((1, w), lambda i: (0, 0))
    return pl.pallas_call(
        _s5_glu_kernel,
        grid=(s // tm,),
        in_specs=[pl.BlockSpec((tm, w), lambda i: (i, 0)),
                  pl.BlockSpec((tm, w), lambda i: (i, COL_SU // S_WIDTH)),
                  row, pl.BlockSpec((w, w), lambda i: (0, 0)), row],
        out_specs=pl.BlockSpec((tm, w), lambda i: (i, 0)),
        out_shape=jax.ShapeDtypeStruct((s, w), BF16),
        compiler_params=_params("arbitrary"),
    )(y, proj, d_skip, w_glu, b_glu)


def _s5(proj, ops, d_skip, w_glu, b_glu):
    s = proj.shape[0]
    T, C = S_T, S_GROUP
    u = proj[:, COL_SU:COL_SU + S_WIDTH].astype(BF16)
    u = u.reshape(s // T, T, S_PAIRS, 2, C).transpose(2, 0, 3, 1, 4).reshape(S_PAIRS, s // T, 2 * T * C)
    y = _s5_scan(u, ops)
    y = y.reshape(S_PAIRS, s // T, 2, T, C).transpose(1, 3, 0, 2, 4).reshape(s, S_WIDTH)
    return _s5_glu(y, proj, d_skip, w_glu, b_glu)


def _rope_tab_kernel(pos_ref, inv_ref, sgn_ref, cos_ref, sin_ref):
    ang = pos_ref[...].astype(F32) * inv_ref[...]
    cos_ref[...] = jnp.cos(ang)
    sin_ref[...] = jnp.sin(ang) * sgn_ref[...]


def _rope_tables(pos_col):
    s = pos_col.shape[0]
    half = N_HEAD_DIM // 2
    inv = (ROPE_THETA ** (-jnp.arange(half, dtype=F32) / half))
    inv = jnp.concatenate([inv, inv]).reshape(1, N_HEAD_DIM)
    sgn = jnp.concatenate([-jnp.ones((half,), F32), jnp.ones((half,), F32)]).reshape(1, N_HEAD_DIM)
    tm = min(s, 2048)
    row = pl.BlockSpec((1, N_HEAD_DIM), lambda i: (0, 0))
    tab = pl.BlockSpec((tm, N_HEAD_DIM), lambda i: (i, 0))
    return pl.pallas_call(
        _rope_tab_kernel,
        grid=(s // tm,),
        in_specs=[pl.BlockSpec((tm, 1), lambda i: (i, 0)), row, row],
        out_specs=[tab, tab],
        out_shape=[jax.ShapeDtypeStruct((s, N_HEAD_DIM), F32)] * 2,
        compiler_params=_params("arbitrary"),
    )(pos_col, inv, sgn)


def _rope(x, cos_f, sin_f):
    return x * cos_f + pltpu.roll(x, N_HEAD_DIM // 2, axis=1) * sin_f


def _nsa_pre_kernel(*refs):
    kv = refs[0:6 * N_KV_HEADS]
    cos_ref, sin_ref = refs[12], refs[13]
    kcmp_o, vcmp_o, kslc_o, vslct_o, kwin_o, vwint_o = refs[14:]
    cos_f, sin_f = cos_ref[...], sin_ref[...]
    for kh in range(N_KV_HEADS):
        kcmp_o[kh] = _rope(kv[0 * N_KV_HEADS + kh][...], cos_f, sin_f)
        vcmp_o[kh] = kv[1 * N_KV_HEADS + kh][...]
        kslc_o[kh] = _rope(kv[2 * N_KV_HEADS + kh][...], cos_f, sin_f).astype(BF16)
        vslct_o[kh] = kv[3 * N_KV_HEADS + kh][...].T.astype(BF16)
        kwin_o[kh] = _rope(kv[4 * N_KV_HEADS + kh][...], cos_f, sin_f).astype(BF16)
        vwint_o[kh] = kv[5 * N_KV_HEADS + kh][...].T.astype(BF16)


def _nsa_pre(proj, cos_f, sin_f):
    s = proj.shape[0]
    Hk, Dh = N_KV_HEADS, N_HEAD_DIM
    tm = 512
    kv_specs = [pl.BlockSpec((tm, Dh), functools.partial(lambda i, c: (i, c), c=COL_NKV // Dh + n))
                for n in range(6 * Hk)]
    tab = pl.BlockSpec((tm, Dh), lambda i: (i, 0))
    std = pl.BlockSpec((Hk, tm, Dh), lambda i: (0, i, 0))
    tr = pl.BlockSpec((Hk, Dh, tm), lambda i: (0, 0, i))
    return pl.pallas_call(
        _nsa_pre_kernel,
        grid=(s // tm,),
        in_specs=kv_specs + [tab, tab],
        out_specs=[std, std, std, tr, std, tr],
        out_shape=[jax.ShapeDtypeStruct((Hk, s, Dh), F32), jax.ShapeDtypeStruct((Hk, s, Dh), F32),
                   jax.ShapeDtypeStruct((Hk, s, Dh), BF16), jax.ShapeDtypeStruct((Hk, Dh, s), BF16),
                   jax.ShapeDtypeStruct((Hk, s, Dh), BF16), jax.ShapeDtypeStruct((Hk, Dh, s), BF16)],
        compiler_params=_params("arbitrary"),
    )(*([proj] * (6 * Hk)), cos_f, sin_f)


def _cmp_kernel(x_ref, p1_ref, p2_ref, wt_ref, wb_ref, w2_ref, o_ref):
    x = x_ref[...]
    first = _dot((x + p1_ref[...]).astype(BF16), wt_ref[...])
    second = _dot((x + p2_ref[...]).astype(BF16), wb_ref[...])
    npieces = x.shape[0]
    pre = first + pltpu.roll(second, npieces - 1, axis=0)
    o_ref[...] = _dot(_gelu(pre).astype(BF16), w2_ref[...])


def _compress(x, pos, w1, w2):
    hk, s, dh = x.shape
    npieces = s // CMP_STRIDE
    half = CMP_STRIDE * dh
    xp = x.reshape(hk, npieces, half)
    p1 = pos[:CMP_STRIDE].reshape(1, half)
    p2 = pos[CMP_STRIDE:].reshape(1, half)
    full = lambda shape: pl.BlockSpec(shape, lambda h: (0,) * len(shape))
    return pl.pallas_call(
        _cmp_kernel,
        grid=(hk,),
        in_specs=[pl.BlockSpec((None, npieces, half), lambda h: (h, 0, 0)),
                  full((1, half)), full((1, half)), full((half, dh)), full((half, dh)), full((dh, dh))],
        out_specs=pl.BlockSpec((None, npieces, dh), lambda h: (h, 0, 0)),
        out_shape=jax.ShapeDtypeStruct((hk, npieces, dh), F32),
        compiler_params=_params("arbitrary"),
    )(xp, p1, p2, w1[:half].astype(BF16), w1[half:].astype(BF16), w2.astype(BF16))


def _nsa_attn_kernel(q_ref, sm_ref, cos_ref, sin_ref, kc_ref, vct_ref, ks_ref, vst_ref, kw_ref, vwt_ref,
                     y_ref, gate_s, bias_s, m_s, l_s, acc_s, *, seq):
    Dh, QB, G, TK = N_HEAD_DIM, Q_BLOCK, N_REP, SLC_TILE
    n_slc = seq // SLC_BLOCK
    n_cmp = seq // CMP_STRIDE
    ratio = SLC_BLOCK // CMP_STRIDE
    kh = pl.program_id(0)
    qb = pl.program_id(1)
    t0 = qb * QB
    scale = Dh ** -0.5
    cos_f, sin_f = cos_ref[...], sin_ref[...]
    q = q_ref[...]
    q_t = jnp.concatenate(
        [_rope(q[:, g * Dh:(g + 1) * Dh], cos_f, sin_f).T.astype(BF16) for g in range(G)], axis=1)
    gate_s[...] = jax.nn.sigmoid(sm_ref[...]).T
    t_lane = t0 + lax.broadcasted_iota(jnp.int32, (1, QB), 1)

    s_all = _dot(kc_ref[...], q_t) * scale
    rho = lax.broadcasted_iota(jnp.int32, (n_cmp, QB), 0)
    slab = jnp.zeros_like(rho)
    for r in range(1, ratio):
        slab = slab + (rho >= r * n_slc).astype(jnp.int32)
    j_cmp = (rho - slab * n_slc) * ratio + slab
    valid = (j_cmp * CMP_STRIDE + (CMP_BLOCK - 1)) <= t_lane
    validf = valid.astype(F32)
    pk = jnp.zeros((n_cmp, QB), F32)
    p_parts = []
    for g in range(G):
        s_g = jnp.where(valid, s_all[:, g * QB:(g + 1) * QB], NEG_INF)
        e = jnp.exp(s_g - jnp.max(s_g, axis=0, keepdims=True))
        p = (e / jnp.sum(e, axis=0, keepdims=True)) * validf
        pk = pk + p
        p_parts.append(p.astype(BF16))
    o_cmp = _dot(vct_ref[...], jnp.concatenate(p_parts, axis=1))

    slabs = [pk[r * n_slc:(r + 1) * n_slc] for r in range(ratio)]
    row_a = lax.broadcasted_iota(jnp.int32, (n_slc, QB), 0)
    prev_last = jnp.where(row_a == 0, 0.0, pltpu.roll(slabs[ratio - 1], 1, axis=0))
    mid = slabs[0]
    for r in range(1, ratio - 1):
        mid = mid + slabs[r]
    imp = (prev_last + 2.0 * mid) + slabs[ratio - 1]
    tb = lax.shift_right_logical(t_lane, 6)
    allowed = row_a <= tb
    forced = (row_a == 0) | (row_a == tb) | (row_a == tb - 1)
    score = jnp.where(forced, FORCED_SCORE, jnp.where(allowed, imp, -jnp.inf))
    row_f = row_a.astype(F32)
    sel = jnp.zeros((n_slc, QB), F32)
    for _ in range(min(SLC_TOPK, n_slc)):
        best = jnp.max(score, axis=0, keepdims=True)
        idx = jnp.min(jnp.where(score == best, row_f, float(n_slc)), axis=0, keepdims=True)
        hit = row_f == idx
        sel = jnp.where(hit, 1.0, sel)
        score = jnp.where(hit, -jnp.inf, score)
    bias_s[...] = jnp.where(sel > 0.0, 0.0, NEG_INF)

    m_s[...] = jnp.full(m_s.shape, NEG_INF, F32)
    l_s[...] = jnp.zeros_like(l_s)
    acc_s[...] = jnp.zeros_like(acc_s)
    row_k = lax.broadcasted_iota(jnp.int32, (TK, QB), 0)
    blocks_per_tile = TK // SLC_BLOCK

    def slc_tile(kt, carry):
        k0 = pl.multiple_of(kt * TK, TK)
        s = _dot(ks_ref[pl.ds(k0, TK), :], q_t) * scale
        b8 = bias_s[pl.ds(pl.multiple_of(kt * blocks_per_tile, blocks_per_tile), blocks_per_tile), :]
        bias = jnp.broadcast_to(b8[:, None, :], (blocks_per_tile, SLC_BLOCK, QB)).reshape(TK, QB)
        bias = jnp.where(k0 + row_k <= t_lane, bias, NEG_INF)
        s = s + jnp.concatenate([bias] * G, axis=1)
        m_old = m_s[...]
        m_new = jnp.maximum(m_old, jnp.max(s, axis=0, keepdims=True))
        alpha = jnp.exp(m_old - m_new)
        p = jnp.exp(s - m_new)
        l_s[...] = alpha * l_s[...] + jnp.sum(p, axis=0, keepdims=True)
        m_s[...] = m_new
        acc_s[...] = alpha * acc_s[...] + _dot(vst_ref[:, pl.ds(k0, TK)], p.astype(BF16))
        return carry

    lax.fori_loop(0, (t0 + QB + TK - 1) // TK, slc_tile, 0)
    o_slc = acc_s[...] / l_s[...]

    WK = WINDOW + QB
    w0 = pl.multiple_of(jnp.maximum(t0 - WINDOW, 0), QB)
    s = _dot(kw_ref[pl.ds(w0, WK), :], q_t) * scale
    wpos = w0 + lax.broadcasted_iota(jnp.int32, (WK, QB), 0)
    wbias = jnp.where(wpos <= t_lane, jnp.where(wpos > t_lane - WINDOW, 0.0, NEG_INF), NEG_INF)
    s = s + jnp.concatenate([wbias] * G, axis=1)
    p = jnp.exp(s - jnp.max(s, axis=0, keepdims=True))
    o_win = _dot(vwt_ref[:, pl.ds(w0, WK)], p.astype(BF16)) / jnp.sum(p, axis=0, keepdims=True)

    outs = []
    for g in range(G):
        base = SMALL_G + (kh * G + g) * 3
        sl = slice(g * QB, (g + 1) * QB)
        o = (gate_s[pl.ds(base, 1), :] * o_cmp[:, sl] + gate_s[pl.ds(base + 1, 1), :] * o_slc[:, sl]
             + gate_s[pl.ds(base + 2, 1), :] * o_win[:, sl])
        outs.append(o.T)
    y_ref[...] = jnp.concatenate(outs, axis=1).astype(BF16)


def _nsa_attn(proj, cos_f, sin_f, kc, vct, ks, vst, kw, vwt):
    s = proj.shape[0]
    Hk, Dh, QB, G = N_KV_HEADS, N_HEAD_DIM, Q_BLOCK, N_REP
    n_cmp = s // CMP_STRIDE
    head = lambda a, b: pl.BlockSpec((None, a, b), lambda kh, qb: (kh, 0, 0))
    tab = pl.BlockSpec((QB, Dh), lambda kh, qb: (qb, 0))
    return pl.pallas_call(
        functools.partial(_nsa_attn_kernel, seq=s),
        grid=(Hk, s // QB),
        in_specs=[pl.BlockSpec((QB, G * Dh), lambda kh, qb: (qb, COL_NQ // (G * Dh) + kh)),
                  pl.BlockSpec((QB, 128), lambda kh, qb: (qb, COL_SMALL // 128)),
                  tab, tab,
                  head(n_cmp, Dh), head(Dh, n_cmp), head(s, Dh), head(Dh, s), head(s, Dh), head(Dh, s)],
        out_specs=pl.BlockSpec((QB, G * Dh), lambda kh, qb: (qb, kh)),
        out_shape=jax.ShapeDtypeStruct((s, N_WIDTH), BF16),
        scratch_shapes=[pltpu.VMEM((128, QB), F32),
                        pltpu.VMEM((s // SLC_BLOCK, QB), F32),
                        pltpu.VMEM((1, G * QB), F32),
                        pltpu.VMEM((1, G * QB), F32),
                        pltpu.VMEM((Dh, G * QB), F32)],
        compiler_params=_params("arbitrary", "arbitrary"),
    )(proj, proj, cos_f, sin_f, kc, vct, ks, vst, kw, vwt)


def _slab_major(a):
    hk, n, dh = a.shape
    ratio = SLC_BLOCK // CMP_STRIDE
    return a.reshape(hk, n // ratio, ratio, dh).transpose(0, 2, 1, 3).reshape(hk, n, dh)


def _nsa(proj, cos_f, sin_f, cmp_pos, cmp_w1, cmp_w2):
    kcmp, vcmp, ks, vst, kw, vwt = _nsa_pre(proj, cos_f, sin_f)
    kc = _slab_major(_compress(kcmp, cmp_pos[0], cmp_w1[0], cmp_w2[0])).astype(BF16)
    vc = _slab_major(_compress(vcmp, cmp_pos[1], cmp_w1[1], cmp_w2[1])).astype(BF16)
    return _nsa_attn(proj, cos_f, sin_f, kc, jnp.swapaxes(vc, 1, 2), ks, vst, kw, vwt)


def _outproj_kernel(ym_ref, ys_ref, yn_ref, w_ref, x_ref, g_ref, gt_ref, o_ref):
    y = (_dot(ym_ref[...], w_ref[0:M_WIDTH, :]) + _dot(ys_ref[...], w_ref[M_WIDTH:M_WIDTH + S_WIDTH, :])
         + _dot(yn_ref[...], w_ref[M_WIDTH + S_WIDTH:, :]))
    o_ref[...] = x_ref[...] + gt_ref[...] * ((y * _rms(y)) * g_ref[...])


def _outproj(ym, ys, yn, w, x, g, gt):
    s, d = x.shape
    tm = 512
    row = pl.BlockSpec((1, d), lambda i: (0, 0))
    rows = lambda width: pl.BlockSpec((tm, width), lambda i: (i, 0))
    return pl.pallas_call(
        _outproj_kernel,
        grid=(s // tm,),
        in_specs=[rows(M_WIDTH), rows(S_WIDTH), rows(N_WIDTH), pl.BlockSpec((d, d), lambda i: (0, 0)),
                  rows(d), row, row],
        out_specs=rows(d),
        out_shape=jax.ShapeDtypeStruct((s, d), F32),
        compiler_params=_params("arbitrary"),
    )(ym, ys, yn, w, x, g, gt)


def _ffn_kernel(x_ref, g_ref, sc_ref, sh_ref, wa_ref, wu_ref, cw_ref, cb_ref, wd_ref, gp_ref, gt_ref,
                o_ref, h_ref, acc_ref, carry_ref):
    i = pl.program_id(0)
    j = pl.program_id(1)
    tm = x_ref.shape[0]

    @pl.when(j == 0)
    def _():
        x = x_ref[...]
        h = (x * _rms(x)) * g_ref[...]
        h_ref[...] = (h * (1.0 + sc_ref[...]) + sh_ref[...]).astype(BF16)
        acc_ref[...] = jnp.zeros_like(acc_ref)

    @pl.when(i == 0)
    def _():
        carry_ref[j] = jnp.zeros(carry_ref.shape[1:], F32)

    h = h_ref[...]
    a = _dot(h, wa_ref[...])
    u = _dot(h, wu_ref[...])
    prev = carry_ref[j]
    carry_ref[j] = a[tm - 8:tm, :]
    row = lax.broadcasted_iota(jnp.int32, a.shape, 0)
    a1 = jnp.where(row == 0, prev[7:8, :], pltpu.roll(a, 1, axis=0))
    a2 = jnp.where(row == 0, prev[6:7, :], jnp.where(row == 1, prev[7:8, :], pltpu.roll(a, 2, axis=0)))
    cw = cw_ref[...]
    conv = cw[0:1] * a2 + cw[1:2] * a1 + cw[2:3] * a + cb_ref[...]
    acc_ref[...] += _dot((_gelu(conv) * u).astype(BF16), wd_ref[...])

    @pl.when(j == pl.num_programs(1) - 1)
    def _():
        y = acc_ref[...]
        o_ref[...] = x_ref[...] + gt_ref[...] * ((y * _rms(y)) * gp_ref[...])


def _ffn(x, g, sc, sh, wa, wu, cw, cb, wd, gp, gt):
    s, d = x.shape
    fp = wa.shape[1]
    tm, tf = 512, FFN_TILE
    nj = fp // tf
    row = pl.BlockSpec((1, d), lambda i, j: (0, 0))
    return pl.pallas_call(
        _ffn_kernel,
        grid=(s // tm, nj),
        in_specs=[pl.BlockSpec((tm, d), lambda i, j: (i, 0)), row, row, row,
                  pl.BlockSpec((d, tf), lambda i, j: (0, j)),
                  pl.BlockSpec((d, tf), lambda i, j: (0, j)),
                  pl.BlockSpec((3, tf), lambda i, j: (0, j)),
                  pl.BlockSpec((1, tf), lambda i, j: (0, j)),
                  pl.BlockSpec((tf, d), lambda i, j: (j, 0)),
                  row, row],
        out_specs=pl.BlockSpec((tm, d), lambda i, j: (i, 0)),
        out_shape=jax.ShapeDtypeStruct((s, d), F32),
        scratch_shapes=[pltpu.VMEM((tm, d), BF16), pltpu.VMEM((tm, d), F32), pltpu.VMEM((nj, 8, tf), F32)],
        compiler_params=_params("arbitrary", "arbitrary"),
    )(x, g, sc, sh, wa, wu, cw, cb, wd, gp, gt)


def _proj_weight(w_in):
    d = w_in.shape[0]
    offs = np.cumsum([0, 2 * M_WIDTH, M_WIDTH, M_WIDTH, M_HEADS, M_HEADS, S_WIDTH, N_WIDTH,
                      6 * N_KV_HEADS * N_HEAD_DIM, 3 * N_HEADS])
    seg = [w_in[:, offs[k]:offs[k + 1]] for k in range(9)]
    m_qk, m_v, m_o, m_i, m_f, s_u, n_q, n_kv, n_g = seg
    small = jnp.concatenate([m_i, m_f, n_g], axis=1)
    pad = jnp.zeros((d, N_PROJ_PAD - COL_SMALL - small.shape[1]), w_in.dtype)
    return jnp.concatenate([m_qk, m_v, m_o, s_u, n_q, n_kv, small, pad], axis=1).astype(BF16)


def _pad_cols(a, n):
    return jnp.concatenate([a, jnp.zeros(a.shape[:-1] + (n - a.shape[-1],), a.dtype)], axis=-1)


def kernel(x, c, positions, w_ada, b_ada, g_pre_mix, g_post_mix, g_pre_ffn, g_post_ffn, w_in, w_out, m_conv_w, m_conv_b, m_i_bias, m_f_bias, m_norm_g, s_lam_re, s_lam_im, s_log_dt, s_b_re, s_b_im, s_c_re, s_c_im, s_d, s_w_glu, s_b_glu, n_cmp_pos, n_cmp_w1, n_cmp_w2, f_w_up, f_conv_w, f_conv_b, f_w_down):
    batch, seq, d = x.shape
    assert batch == 1 and d == D_MODEL and seq % 1024 == 0
    depth = w_ada.shape[0]
    xs = x.reshape(seq, d)
    mod = _ada_mod(c.reshape(d, 1), w_ada, b_ada)
    cos_f, sin_f = _rope_tables(positions.reshape(seq, 1))
    row = lambda a: a.reshape(1, -1)
    for l in range(depth):
        sh1, sc1, gt1, sh2, sc2, gt2 = [mod[l, :, k * d:(k + 1) * d] for k in range(6)]
        proj = _inproj(xs, row(g_pre_mix[l]), sc1, sh1, _proj_weight(w_in[l]))
        gate_bias = _pad_cols(jnp.concatenate([m_i_bias[l], m_f_bias[l]]).reshape(1, -1), 128)
        y_m = _mlstm(proj, m_conv_w[l], row(m_conv_b[l]), gate_bias, row(m_norm_g[l]))
        ops = _s5_prep(s_lam_re[l], s_lam_im[l], s_log_dt[l], s_b_re[l], s_b_im[l], s_c_re[l], s_c_im[l])
        y_s = _s5(proj, ops, row(s_d[l]), s_w_glu[l].astype(BF16), row(s_b_glu[l]))
        y_n = _nsa(proj, cos_f, sin_f, n_cmp_pos[l], n_cmp_w1[l], n_cmp_w2[l])
        xs = _outproj(y_m, y_s, y_n, w_out[l].astype(BF16), xs, row(g_post_mix[l]), gt1)
        wa = _pad_cols(f_w_up[l][:, :D_FF], D_FF_PAD).astype(BF16)
        wu = _pad_cols(f_w_up[l][:, D_FF:], D_FF_PAD).astype(BF16)
        wd = jnp.concatenate([f_w_down[l], jnp.zeros((D_FF_PAD - D_FF, d), F32)], axis=0).astype(BF16)
        xs = _ffn(xs, row(g_pre_ffn[l]), sc2, sh2, wa, wu, _pad_cols(f_conv_w[l], D_FF_PAD),
                  _pad_cols(row(f_conv_b[l]), D_FF_PAD), wd, row(g_post_ffn[l]), gt2)
    return xs.reshape(batch, seq, d)
```

```python
import functools

import numpy as np
import jax
import jax.numpy as jnp
from jax import lax
from jax.experimental import pallas as pl
from jax.experimental.pallas import tpu as pltpu

F32 = jnp.float32
BF16 = jnp.bfloat16

D_MODEL = 2048
EPS = 1e-6
M_WIDTH = 512
M_HEADS = 4
M_HEAD_DIM = 128
M_CHUNK = 128
M_CONV = 4
S_WIDTH = 512
S_GROUP = 16
S_GROUPS = 32
S_STATE = 64
S_T = 16
S_OCT = 8
N_WIDTH = 1024
N_HEAD_DIM = 128
N_HEADS = 8
N_KV_HEADS = 2
N_REP = N_HEADS // N_KV_HEADS
CMP_BLOCK = 32
CMP_STRIDE = 16
SLC_BLOCK = 64
SLC_TOPK = 16
WINDOW = 512
Q_BLOCK = 128
ROPE_THETA = 10000.0
NEG_INF = -1e30
FORCED_SCORE = 1e9
SLC_TILE = 512
D_FF = 5504
D_FF_PAD = 5632
FFN_TILE = 512
COL_MQK, COL_MV, COL_MO, COL_SU, COL_NQ, COL_NKV, COL_SMALL = 0, 1024, 1536, 2048, 2560, 3584, 5120
N_PROJ_PAD = 5376
PROJ_TILE = 768
SMALL_I, SMALL_F, SMALL_G = 0, 4, 8

VMEM_LIMIT_BYTES = 58 * 1024 * 1024


def _params(*sem):
    return pltpu.CompilerParams(dimension_semantics=sem, vmem_limit_bytes=VMEM_LIMIT_BYTES)


def _gelu(x):
    return 0.5 * x * (1.0 + jnp.tanh(0.7978845608028654 * (x + 0.044715 * (x * x * x))))


def _rms(x):
    return lax.rsqrt(jnp.mean(x * x, axis=-1, keepdims=True) + EPS)


def _dot(a, b):
    return jnp.dot(a, b, preferred_element_type=F32)


def _dot_nt(a, b):
    return lax.dot_general(a, b, (((1,), (1,)), ((), ())), preferred_element_type=F32)


def _ada_kernel(c_ref, w_ref, b_ref, o_ref):
    c = c_ref[...]
    cs = c * jax.nn.sigmoid(c)
    o_ref[...] = jnp.sum(w_ref[...] * cs, axis=0, keepdims=True) + b_ref[...]


def _ada_mod(c_col, w_ada, b_ada):
    depth, d, n = w_ada.shape
    tn = 1024
    return pl.pallas_call(
        _ada_kernel,
        grid=(depth, n // tn),
        in_specs=[pl.BlockSpec((d, 1), lambda l, j: (0, 0)),
                  pl.BlockSpec((None, d, tn), lambda l, j: (l, 0, j)),
                  pl.BlockSpec((None, 1, tn), lambda l, j: (l, 0, j))],
        out_specs=pl.BlockSpec((None, 1, tn), lambda l, j: (l, 0, j)),
        out_shape=jax.ShapeDtypeStruct((depth, 1, n), F32),
        compiler_params=_params("arbitrary", "arbitrary"),
    )(c_col, w_ada, b_ada.reshape(depth, 1, n))


def _inproj_kernel(x_ref, g_ref, sc_ref, sh_ref, w_ref, o_ref, h_ref):
    @pl.when(pl.program_id(1) == 0)
    def _():
        x = x_ref[...]
        h = (x * _rms(x)) * g_ref[...]
        h_ref[...] = (h * (1.0 + sc_ref[...]) + sh_ref[...]).astype(BF16)

    o_ref[...] = _dot(h_ref[...], w_ref[...])


def _inproj(x, g, sc, sh, w):
    s, d = x.shape
    n = w.shape[1]
    tm, tn = 512, PROJ_TILE
    row = pl.BlockSpec((1, d), lambda i, j: (0, 0))
    return pl.pallas_call(
        _inproj_kernel,
        grid=(s // tm, n // tn),
        in_specs=[pl.BlockSpec((tm, d), lambda i, j: (i, 0)), row, row, row,
                  pl.BlockSpec((d, tn), lambda i, j: (0, j))],
        out_specs=pl.BlockSpec((tm, tn), lambda i, j: (i, j)),
        out_shape=jax.ShapeDtypeStruct((s, n), F32),
        scratch_shapes=[pltpu.VMEM((tm, d), BF16)],
        compiler_params=_params("arbitrary", "arbitrary"),
    )(x, g, sc, sh, w)


def _mlstm_kernel(qk_ref, v_ref, o_ref, sm_ref, cw_ref, cb_ref, gb_ref, ng_ref, y_ref,
                  tail_ref, c_ref, n_ref, m_ref):
    L, Dh = M_CHUNK, M_HEAD_DIM

    @pl.when(pl.program_id(0) == 0)
    def _():
        tail_ref[...] = jnp.zeros_like(tail_ref)
        c_ref[...] = jnp.zeros_like(c_ref)
        n_ref[...] = jnp.zeros_like(n_ref)
        m_ref[...] = jnp.zeros_like(m_ref)

    raw = qk_ref[...]
    tail = tail_ref[...]
    row = lax.broadcasted_iota(jnp.int32, raw.shape, 0)

    def shifted(d):
        r = pltpu.roll(raw, d, axis=0)
        for k in range(d):
            r = jnp.where(row == k, tail[8 - d + k:8 - d + k + 1, :], r)
        return r

    cw = cw_ref[...]
    pre = (cw[3:4] * raw + cw[2:3] * shifted(1) + cw[1:2] * shifted(2) + cw[0:1] * shifted(3)
           + cb_ref[...])
    tail_ref[...] = raw[L - 8:L, :]
    qk = pre * jax.nn.sigmoid(pre)

    sm = sm_ref[...] + gb_ref[...]
    lf = jnp.minimum(sm, 0.0) - jnp.log1p(jnp.exp(-jnp.abs(sm)))
    rowg = lax.broadcasted_iota(jnp.int32, lf.shape, 0)
    g = lf
    for s in (1, 2, 4, 8, 16, 32, 64):
        g = g + jnp.where(rowg >= s, pltpu.roll(g, s, axis=0), 0.0)
    smT = sm.T
    gT = g.T
    ri = lax.broadcasted_iota(jnp.int32, (L, L), 0)
    ci = lax.broadcasted_iota(jnp.int32, (L, L), 1)
    causal = ri >= ci

    for h in range(M_HEADS):
        hs = slice(h * Dh, (h + 1) * Dh)
        q = qk[:, h * Dh:(h + 1) * Dh]
        k = qk[:, M_WIDTH + h * Dh:M_WIDTH + (h + 1) * Dh] * (Dh ** -0.5)
        v = v_ref[:, hs]
        i_c = sm[:, SMALL_I + h:SMALL_I + h + 1]
        g_c = g[:, SMALL_F + h:SMALL_F + h + 1]
        i_r = smT[SMALL_I + h:SMALL_I + h + 1, :]
        g_r = gT[SMALL_F + h:SMALL_F + h + 1, :]
        g_last = g_r[:, L - 1:L]
        a_r = g_last - g_r + i_r
        a_max = jnp.max(a_r, axis=1, keepdims=True)
        w_c = jnp.exp(g_last - g_c + i_c - a_max)
        m_prev = m_ref[h:h + 1, 0:1]
        c_prev = c_ref[h]
        n_prev = n_ref[h:h + 1, :]
        qb = q.astype(BF16)
        kb = k.astype(BF16)
        vb = v.astype(BF16)
        c_loc = _dot(k.T.astype(BF16), (w_c * v).astype(BF16))
        n_loc = jnp.sum(w_c * k, axis=0, keepdims=True)
        m_new = jnp.maximum(g_last + m_prev, a_max)
        s_old = jnp.exp(g_last + m_prev - m_new)
        s_new = jnp.exp(a_max - m_new)
        c_ref[h] = s_old * c_prev + s_new * c_loc
        n_ref[h:h + 1, :] = s_old * n_prev + s_new * n_loc
        m_ref[h:h + 1, :] = jnp.broadcast_to(m_new, (1, Dh))
        dlog = jnp.where(causal, g_c - g_r + i_r, -jnp.inf)
        m_inter = g_c + m_prev
        m_t = jnp.maximum(m_inter, jnp.max(dlog, axis=1, keepdims=True))
        sw = _dot_nt(qb, kb) * jnp.exp(dlog - m_t)
        inter = jnp.exp(m_inter - m_t)
        num = _dot(sw.astype(BF16), vb) + inter * _dot(qb, c_prev.astype(BF16))
        den = jnp.sum(sw, axis=1, keepdims=True) + inter * jnp.sum(q * n_prev, axis=1, keepdims=True)
        hh = num / jnp.maximum(jnp.abs(den), jnp.exp(-m_t))
        hn = (hh * _rms(hh)) * ng_ref[:, hs]
        y_ref[:, hs] = (jax.nn.sigmoid(o_ref[:, hs]) * hn).astype(BF16)


def _mlstm(proj, conv_w, conv_b, gate_bias, norm_g):
    s = proj.shape[0]
    L = M_CHUNK
    full = lambda shape: pl.BlockSpec(shape, lambda c: (0,) * len(shape))
    return pl.pallas_call(
        _mlstm_kernel,
        grid=(s // L,),
        in_specs=[pl.BlockSpec((L, 2 * M_WIDTH), lambda c: (c, COL_MQK // (2 * M_WIDTH))),
                  pl.BlockSpec((L, M_WIDTH), lambda c: (c, COL_MV // M_WIDTH)),
                  pl.BlockSpec((L, M_WIDTH), lambda c: (c, COL_MO // M_WIDTH)),
                  pl.BlockSpec((L, 128), lambda c: (c, COL_SMALL // 128)),
                  full((M_CONV, 2 * M_WIDTH)), full((1, 2 * M_WIDTH)), full((1, 128)), full((1, M_WIDTH))],
        out_specs=pl.BlockSpec((L, M_WIDTH), lambda c: (c, 0)),
        out_shape=jax.ShapeDtypeStruct((s, M_WIDTH), BF16),
        scratch_shapes=[pltpu.VMEM((8, 2 * M_WIDTH), F32),
                        pltpu.VMEM((M_HEADS, M_HEAD_DIM, M_HEAD_DIM), F32),
                        pltpu.VMEM((8, M_HEAD_DIM), F32),
                        pltpu.VMEM((8, M_HEAD_DIM), F32)],
        compiler_params=_params("arbitrary"),
    )(proj, proj, proj, proj, conv_w, conv_b, gate_bias, norm_g)


def _split3(a):
    hi = a.astype(BF16)
    r = a - hi.astype(F32)
    mid = r.astype(BF16)
    lo = (r - mid.astype(F32)).astype(BF16)
    return hi, mid, lo


def _dot_f32(a, b):
    ah, am, al = _split3(a)
    bh, bm, bl = _split3(b)
    small = _dot(ah, bl) + _dot(am, bm) + _dot(al, bh)
    mid = _dot(ah, bm) + _dot(am, bh)
    return _dot(ah, bh) + (mid + small)


def _s5_prep_kernel(lrr_ref, lir_ref, lrc_ref, lic_ref, ldt_ref, btr_ref, bti_ref, ctr_ref, cti_ref,
                    m_ref, pr_ref, pi_ref, rr_ref, ri_ref, ar_ref, ai_ref):
    T = S_T
    dt = jnp.exp(ldt_ref[...])
    lr, li = lrr_ref[...], lir_ref[...]
    mag = jnp.exp(lr * dt)
    a_re = mag * jnp.cos(li * dt)
    a_im = mag * jnp.sin(li * dt)
    den = lr * lr + li * li
    nr = a_re - 1.0
    z_re = (nr * lr + a_im * li) / den
    z_im = (a_im * lr - nr * li) / den
    btr, bti = btr_ref[...], bti_ref[...]
    bb_re = z_re * btr - z_im * bti
    bb_im = z_re * bti + z_im * btr

    def pow_row(e):
        mg = jnp.exp((lr * dt) * e)
        th = (li * dt) * e
        return mg * jnp.cos(th), mg * jnp.sin(th)

    jrow = lax.shift_right_logical(lax.broadcasted_iota(jnp.int32, btr.shape, 0), 4).astype(F32)
    p_re, p_im = pow_row(-jrow)
    left_re = bb_re * p_re - bb_im * p_im
    left_im = bb_re * p_im + bb_im * p_re
    q_re, q_im = pow_row((T - 1.0) - jrow)
    pr_ref[...] = bb_re * q_re - bb_im * q_im
    pi_ref[...] = bb_re * q_im + bb_im * q_re

    lrc, lic = lrc_ref[...], lic_ref[...]
    ctr, cti = ctr_ref[...], cti_ref[...]

    def pow_col(e):
        mg = jnp.exp((lrc * dt) * e)
        th = (lic * dt) * e
        return mg * jnp.cos(th), mg * jnp.sin(th)

    icol = lax.shift_right_logical(lax.broadcasted_iota(jnp.int32, ctr.shape, 1), 4).astype(F32)
    e_re, e_im = pow_col(icol)
    right_re = ctr * e_re - cti * e_im
    right_im = ctr * e_im + cti * e_re
    m = _dot_f32(left_re, right_re) - _dot_f32(left_im, right_im)
    mj = lax.shift_right_logical(lax.broadcasted_iota(jnp.int32, m.shape, 0), 4)
    mi = lax.shift_right_logical(lax.broadcasted_iota(jnp.int32, m.shape, 1), 4)
    m_ref[...] = jnp.where(mi >= mj, m, 0.0)
    f_re, f_im = pow_col(icol + 1.0)
    rr_ref[...] = ctr * f_re - cti * f_im
    ri_ref[...] = -(ctr * f_im + cti * f_re)
    mag_t = jnp.exp((lr * dt) * T)
    ar_ref[...] = mag_t * jnp.cos((li * dt) * T)
    ai_ref[...] = mag_t * jnp.sin((li * dt) * T)


def _s5_prep(lam_re, lam_im, log_dt, b_re, b_im, c_re, c_im):
    G, N, C, T = S_GROUPS, S_STATE, S_GROUP, S_T
    TC = T * C
    bt = lambda b: jnp.tile(jnp.swapaxes(b, 1, 2), (1, T, 1))
    ct = lambda c: jnp.tile(jnp.swapaxes(c, 1, 2), (1, 1, T))
    blk = lambda a, b: pl.BlockSpec((None, a, b), lambda g: (g, 0, 0))
    outs = pl.pallas_call(
        _s5_prep_kernel,
        grid=(G,),
        in_specs=[blk(1, N), blk(1, N), blk(N, 1), blk(N, 1), blk(1, 1),
                  blk(TC, N), blk(TC, N), blk(N, TC), blk(N, TC)],
        out_specs=[blk(TC, TC), blk(TC, N), blk(TC, N), blk(N, TC), blk(N, TC), blk(1, N), blk(1, N)],
        out_shape=[jax.ShapeDtypeStruct((G, TC, TC), F32),
                   jax.ShapeDtypeStruct((G, TC, N), F32), jax.ShapeDtypeStruct((G, TC, N), F32),
                   jax.ShapeDtypeStruct((G, N, TC), F32), jax.ShapeDtypeStruct((G, N, TC), F32),
                   jax.ShapeDtypeStruct((G, 1, N), F32), jax.ShapeDtypeStruct((G, 1, N), F32)],
        compiler_params=_params("arbitrary"),
    )(lam_re.reshape(G, 1, N), lam_im.reshape(G, 1, N), lam_re.reshape(G, N, 1), lam_im.reshape(G, N, 1),
      log_dt.reshape(G, 1, 1), bt(b_re), bt(b_im), ct(c_re), ct(c_im))
    m, pr, pi, rr, ri, ar, ai = outs

    O, NO = S_OCT, G // S_OCT
    eye = jnp.eye(O, dtype=F32)
    m_oct = (m.reshape(NO, O, T, C, T, 1, C) * eye.reshape(1, O, 1, 1, 1, O, 1)
             ).transpose(0, 2, 1, 3, 4, 5, 6).reshape(NO, T * O * C, T * O * C)
    p_oct = lambda p: (p.reshape(NO, O, T, C, 1, N) * eye.reshape(1, O, 1, 1, O, 1)
                       ).transpose(0, 2, 1, 3, 4, 5).reshape(NO, T * O * C, O * N).astype(BF16)
    r_oct = lambda r: (r.reshape(NO, O, N, T, 1, C) * eye.reshape(1, O, 1, 1, O, 1)
                       ).reshape(NO, O * N, T * O * C).astype(BF16)
    oct_row = lambda a: a.reshape(NO, 1, O * N)
    return (m_oct.astype(BF16), p_oct(pr), p_oct(pi), r_oct(rr), r_oct(ri), oct_row(ar), oct_row(ai))


def _s5_kernel(u_ref, m_ref, pr_ref, pi_ref, rr_ref, ri_ref, ar_ref, ai_ref, y_ref,
               xr_s, xi_s, er_s, ei_s, st_s):
    T = S_T
    nch = u_ref.shape[0] // T

    @pl.when(pl.program_id(1) == 0)
    def _():
        st_s[...] = jnp.zeros_like(st_s)

    u = jnp.concatenate([u_ref[pl.ds(j, nch, stride=T), :] for j in range(T)], axis=1).astype(BF16)
    y_loc = _dot(u, m_ref[...])
    xr_s[...] = _dot(u, pr_ref[...])
    xi_s[...] = _dot(u, pi_ref[...])
    a_re, a_im = ar_ref[...], ai_ref[...]

    def body(b, carry):
        e_re, e_im = carry
        base = pl.multiple_of(b * 8, 8)
        xr = xr_s[pl.ds(base, 8), :]
        xi = xi_s[pl.ds(base, 8), :]
        for k in range(8):
            er_s[pl.ds(base + k, 1), :] = e_re
            ei_s[pl.ds(base + k, 1), :] = e_im
            n_re = a_re * e_re - a_im * e_im + xr[k:k + 1]
            n_im = a_re * e_im + a_im * e_re + xi[k:k + 1]
            e_re, e_im = n_re, n_im
        return e_re, e_im

    e_re, e_im = lax.fori_loop(0, nch // 8, body, (st_s[0:1, :], st_s[1:2, :]))
    st_s[0:1, :] = e_re
    st_s[1:2, :] = e_im
    y = (y_loc + _dot(er_s[...].astype(BF16), rr_ref[...]) + _dot(ei_s[...].astype(BF16), ri_ref[...]))
    lanes = y_ref.shape[1]
    for i in range(T):
        y_ref[pl.ds(i, nch, stride=T), :] = y[:, i * lanes:(i + 1) * lanes]


def _s5_scan(proj, ops):
    m, pr, pi, rr, ri, ar, ai = ops
    s = proj.shape[0]
    no, w, ns = pr.shape
    lanes = S_OCT * S_GROUP
    ts = min(s, 4096)
    nch = ts // S_T
    op = lambda a, b: pl.BlockSpec((None, a, b), lambda o, r: (o, 0, 0))
    return pl.pallas_call(
        _s5_kernel,
        grid=(no, s // ts),
        in_specs=[pl.BlockSpec((ts, lanes), lambda o, r: (r, COL_SU // lanes + o)),
                  op(w, w), op(w, ns), op(w, ns), op(ns, w), op(ns, w), op(1, ns), op(1, ns)],
        out_specs=pl.BlockSpec((None, ts, lanes), lambda o, r: (o, r, 0)),
        out_shape=jax.ShapeDtypeStruct((no, s, lanes), F32),
        scratch_shapes=[pltpu.VMEM((nch, ns), F32) for _ in range(4)] + [pltpu.VMEM((8, ns), F32)],
        compiler_params=_params("arbitrary", "arbitrary"),
    )(proj, m, pr, pi, rr, ri, ar, ai)


def _s5_glu_kernel(y_ref, u_ref, d_ref, w_ref, b_ref, o_ref):
    y = jnp.concatenate([y_ref[o] for o in range(y_ref.shape[0])], axis=1)
    y = _gelu(y + d_ref[...] * u_ref[...])
    z = _dot(y.astype(BF16), w_ref[...]) + b_ref[...]
    o_ref[...] = (y * jax.nn.sigmoid(z)).astype(BF16)


def _s5_glu(y, proj, d_skip, w_glu, b_glu):
    no, s, lanes = y.shape
    w = no * lanes
    tm = 1024
    row = pl.BlockSpec((1, w), lambda i: (0, 0))
    return pl.pallas_call(
        _s5_glu_kernel,
        grid=(s // tm,),
        in_specs=[pl.BlockSpec((no, tm, lanes), lambda i: (0, i, 0)),
                  pl.BlockSpec((tm, w), lambda i: (i, COL_SU // S_WIDTH)),
                  row, pl.BlockSpec((w, w), lambda i: (0, 0)), row],
        out_specs=pl.BlockSpec((tm, w), lambda i: (i, 0)),
        out_shape=jax.ShapeDtypeStruct((s, w), BF16),
        compiler_params=_params("arbitrary"),
    )(y, proj, d_skip, w_glu, b_glu)


def _s5(proj, ops, d_skip, w_glu, b_glu):
    return _s5_glu(_s5_scan(proj, ops), proj, d_skip, w_glu, b_glu)


def _rope_tab_kernel(pos_ref, inv_ref, sgn_ref, cos_ref, sin_ref):
    ang = pos_ref[...].astype(F32) * inv_ref[...]
    cos_ref[...] = jnp.cos(ang)
    sin_ref[...] = jnp.sin(ang) * sgn_ref[...]


def _rope_tables(pos_col):
    s = pos_col.shape[0]
    half = N_HEAD_DIM // 2
    inv = (ROPE_THETA ** (-jnp.arange(half, dtype=F32) / half))
    inv = jnp.concatenate([inv, inv]).reshape(1, N_HEAD_DIM)
    sgn = jnp.concatenate([-jnp.ones((half,), F32), jnp.ones((half,), F32)]).reshape(1, N_HEAD_DIM)
    tm = min(s, 2048)
    row = pl.BlockSpec((1, N_HEAD_DIM), lambda i: (0, 0))
    tab = pl.BlockSpec((tm, N_HEAD_DIM), lambda i: (i, 0))
    return pl.pallas_call(
        _rope_tab_kernel,
        grid=(s // tm,),
        in_specs=[pl.BlockSpec((tm, 1), lambda i: (i, 0)), row, row],
        out_specs=[tab, tab],
        out_shape=[jax.ShapeDtypeStruct((s, N_HEAD_DIM), F32)] * 2,
        compiler_params=_params("arbitrary"),
    )(pos_col, inv, sgn)


def _col_reduce(x, op):
    while x.shape[0] > 8 and x.shape[0] % 64 == 0:
        x = op(x.reshape(8, x.shape[0] // 8, x.shape[1]), axis=0)
    return op(x, axis=0, keepdims=True)


def _colmax(x):
    return _col_reduce(x, jnp.max)


def _colsum(x):
    return _col_reduce(x, jnp.sum)


def _rope(x, cos_f, sin_f):
    return x * cos_f + pltpu.roll(x, N_HEAD_DIM // 2, axis=1) * sin_f


def _nsa_pre_kernel(*refs):
    kv = refs[0:6 * N_KV_HEADS]
    cos_ref, sin_ref = refs[12], refs[13]
    kcmp_o, vcmp_o, kslc_o, vslct_o, kwin_o, vwint_o = refs[14:]
    cos_f, sin_f = cos_ref[...], sin_ref[...]
    tm = cos_f.shape[0]
    key = pl.program_id(0) * tm + lax.broadcasted_iota(jnp.int32, (tm, N_HEAD_DIM), 0)
    blk = lax.shift_right_logical(key, 6) & (SLC_TILE // SLC_BLOCK - 1)
    onehot = jnp.where(lax.broadcasted_iota(jnp.int32, (tm, N_HEAD_DIM), 1) == blk, 1.0, 0.0).astype(BF16)
    for kh in range(N_KV_HEADS):
        kcmp_o[kh] = _rope(kv[0 * N_KV_HEADS + kh][...], cos_f, sin_f)
        vcmp_o[kh] = kv[1 * N_KV_HEADS + kh][...]
        kslc_o[kh] = jnp.concatenate(
            [_rope(kv[2 * N_KV_HEADS + kh][...], cos_f, sin_f).astype(BF16), onehot], axis=1)
        vslct_o[kh] = kv[3 * N_KV_HEADS + kh][...].T.astype(BF16)
        kwin_o[kh] = _rope(kv[4 * N_KV_HEADS + kh][...], cos_f, sin_f).astype(BF16)
        vwint_o[kh] = kv[5 * N_KV_HEADS + kh][...].T.astype(BF16)


def _nsa_pre(proj, cos_f, sin_f):
    s = proj.shape[0]
    Hk, Dh = N_KV_HEADS, N_HEAD_DIM
    tm = 512
    kv_specs = [pl.BlockSpec((tm, Dh), functools.partial(lambda i, c: (i, c), c=COL_NKV // Dh + n))
                for n in range(6 * Hk)]
    tab = pl.BlockSpec((tm, Dh), lambda i: (i, 0))
    std = pl.BlockSpec((Hk, tm, Dh), lambda i: (0, i, 0))
    tr = pl.BlockSpec((Hk, Dh, tm), lambda i: (0, 0, i))
    ext = pl.BlockSpec((Hk, tm, 2 * Dh), lambda i: (0, i, 0))
    return pl.pallas_call(
        _nsa_pre_kernel,
        grid=(s // tm,),
        in_specs=kv_specs + [tab, tab],
        out_specs=[std, std, ext, tr, std, tr],
        out_shape=[jax.ShapeDtypeStruct((Hk, s, Dh), F32), jax.ShapeDtypeStruct((Hk, s, Dh), F32),
                   jax.ShapeDtypeStruct((Hk, s, 2 * Dh), BF16), jax.ShapeDtypeStruct((Hk, Dh, s), BF16),
                   jax.ShapeDtypeStruct((Hk, s, Dh), BF16), jax.ShapeDtypeStruct((Hk, Dh, s), BF16)],
        compiler_params=_params("arbitrary"),
    )(*([proj] * (6 * Hk)), cos_f, sin_f)


def _cmp_kernel(x_ref, p1_ref, p2_ref, wt_ref, wb_ref, w2_ref, o_ref):
    x = x_ref[...]
    first = _dot((x + p1_ref[...]).astype(BF16), wt_ref[...])
    second = _dot((x + p2_ref[...]).astype(BF16), wb_ref[...])
    npieces = x.shape[0]
    pre = first + pltpu.roll(second, npieces - 1, axis=0)
    o_ref[...] = _dot(_gelu(pre).astype(BF16), w2_ref[...])


def _compress(x, pos, w1, w2):
    hk, s, dh = x.shape
    npieces = s // CMP_STRIDE
    half = CMP_STRIDE * dh
    xp = x.reshape(hk, npieces, half)
    p1 = pos[:CMP_STRIDE].reshape(1, half)
    p2 = pos[CMP_STRIDE:].reshape(1, half)
    full = lambda shape: pl.BlockSpec(shape, lambda h: (0,) * len(shape))
    return pl.pallas_call(
        _cmp_kernel,
        grid=(hk,),
        in_specs=[pl.BlockSpec((None, npieces, half), lambda h: (h, 0, 0)),
                  full((1, half)), full((1, half)), full((half, dh)), full((half, dh)), full((dh, dh))],
        out_specs=pl.BlockSpec((None, npieces, dh), lambda h: (h, 0, 0)),
        out_shape=jax.ShapeDtypeStruct((hk, npieces, dh), F32),
        compiler_params=_params("arbitrary"),
    )(xp, p1, p2, w1[:half].astype(BF16), w1[half:].astype(BF16), w2.astype(BF16))


def _nsa_attn_kernel(*refs, seq):
    NW = WINDOW // Q_BLOCK + 1
    q_ref, sm_ref, cos_ref, sin_ref, cend_ref, kc_ref, vct_ref, ks_ref, vst_ref, ksd_ref, vsd_ref = refs[:11]
    kw_refs = refs[11:11 + NW]
    vw_refs = refs[11 + NW:11 + 2 * NW]
    y_ref, gate_s, bias_s, rhs_s, sa_s, sb_s, m_s, l_s, acc_s = refs[11 + 2 * NW:]
    Dh, QB, G, TK = N_HEAD_DIM, Q_BLOCK, N_REP, SLC_TILE
    n_slc = seq // SLC_BLOCK
    n_cmp = seq // CMP_STRIDE
    ratio = SLC_BLOCK // CMP_STRIDE
    kh = pl.program_id(0)
    qb = pl.program_id(1)
    t0 = qb * QB
    c2 = (Dh ** -0.5) * 1.4426950408889634
    cos_f, sin_f = cos_ref[...], sin_ref[...]
    q = q_ref[...]
    q_t = jnp.concatenate(
        [_rope(q[:, g * Dh:(g + 1) * Dh], cos_f, sin_f).T.astype(BF16) for g in range(G)], axis=1)
    gate_s[...] = jax.nn.sigmoid(sm_ref[...]).T
    t_lane = t0 + lax.broadcasted_iota(jnp.int32, (1, QB), 1)

    s_all = _dot(kc_ref[...], q_t)
    valid = cend_ref[...] <= t0
    any_valid = t_lane >= CMP_BLOCK - 1
    pk = jnp.zeros((n_cmp, QB), F32)
    p_parts = []
    for g in range(G):
        s_g = jnp.where(valid, s_all[:, g * QB:(g + 1) * QB], NEG_INF)
        e = jnp.exp2((s_g - _colmax(s_g)) * c2)
        p = e * jnp.where(any_valid, 1.0 / _colsum(e), 0.0)
        pk = pk + p
        p_parts.append(p.astype(BF16))
    o_cmp = _dot(vct_ref[...], jnp.concatenate(p_parts, axis=1))

    s = _dot(jnp.concatenate([r[...] for r in kw_refs], axis=0), q_t)
    wpos = t0 - WINDOW + lax.broadcasted_iota(jnp.int32, (WINDOW + QB, QB), 0)
    wbias = jnp.where(wpos <= t_lane, jnp.where(wpos > t_lane - WINDOW, jnp.where(wpos >= 0, 0.0, NEG_INF),
                                                NEG_INF), NEG_INF)
    s = s + jnp.concatenate([wbias] * G, axis=1)
    p = jnp.exp2((s - _colmax(s)) * c2)
    o_win = _dot(jnp.concatenate([r[...] for r in vw_refs], axis=1), p.astype(BF16)) / _colsum(p)

    rhs_s[0:Dh, :] = q_t
    rhs_s[Dh:2 * Dh, :] = jnp.zeros((Dh, G * QB), BF16)

    slabs = [pk[r * n_slc:(r + 1) * n_slc] for r in range(ratio)]
    row_a = lax.broadcasted_iota(jnp.int32, (n_slc, QB), 0)
    prev_last = jnp.where(row_a == 0, 0.0, pltpu.roll(slabs[ratio - 1], 1, axis=0))
    mid = slabs[0]
    for r in range(1, ratio - 1):
        mid = mid + slabs[r]
    imp = (prev_last + 2.0 * mid) + slabs[ratio - 1]
    tb = lax.shift_right_logical(t_lane, 6)
    allowed = row_a <= tb
    forced = (row_a == 0) | (row_a == tb) | (row_a == tb - 1)
    score = jnp.where(forced, -jnp.inf, jnp.where(allowed, imp, -jnp.inf))
    row_f = row_a.astype(F32)
    for _ in range(max(min(SLC_TOPK, n_slc) - 3, 0)):
        best = _colmax(score)
        idx = _col_reduce(jnp.where(score == best, row_f, float(n_slc)), jnp.min)
        score = jnp.where(row_f == idx, -jnp.inf, score)
    bias_s[...] = jnp.where(score == -jnp.inf, jnp.where(row_a < 2 * qb, 0.0, NEG_INF), NEG_INF)

    blocks_per_tile = TK // SLC_BLOCK
    n_tiles = (t0 + TK - 1) // TK

    def scores(kt, buf):
        live = kt < n_tiles
        kt = jnp.maximum(jnp.minimum(kt, n_tiles - 1), 0)
        b8 = bias_s[pl.ds(pl.multiple_of(kt * blocks_per_tile, blocks_per_tile), blocks_per_tile), :]
        b8 = jnp.where(live, b8, NEG_INF)
        rows = jnp.concatenate([jnp.concatenate([b8] * G, axis=1),
                                jnp.zeros((16 - blocks_per_tile, G * QB), F32)], axis=0)
        rhs_s[Dh:Dh + 16, :] = rows.astype(BF16)
        buf[...] = _dot(ks_ref[pl.ds(pl.multiple_of(kt * TK, TK), TK), :], rhs_s[...])

    def softmax_pv(kt, buf):
        kt = jnp.minimum(kt, n_tiles - 1)
        s = buf[...]
        m_old = m_s[...]
        m_new = jnp.maximum(m_old, _colmax(s))
        alpha = jnp.exp2((m_old - m_new) * c2)
        p = jnp.exp2((s - m_new) * c2)
        l_s[...] = alpha * l_s[...] + _colsum(p)
        m_s[...] = m_new
        acc_s[...] = alpha * acc_s[...] + _dot(vst_ref[:, pl.ds(pl.multiple_of(kt * TK, TK), TK)], p.astype(BF16))

    scores(0, sa_s)

    s = _dot(ksd_ref[:, 0:Dh], q_t)
    kpos = t0 + lax.broadcasted_iota(jnp.int32, (QB, QB), 0)
    s = s + jnp.concatenate([jnp.where(kpos <= t_lane, 0.0, NEG_INF)] * G, axis=1)
    m0 = _colmax(s)
    p = jnp.exp2((s - m0) * c2)
    m_s[...] = m0
    l_s[...] = _colsum(p)
    acc_s[...] = _dot(vsd_ref[...], p.astype(BF16))

    def slc_pair(k2, carry):
        kt = 2 * k2
        scores(kt + 1, sb_s)
        softmax_pv(kt, sa_s)
        scores(kt + 2, sa_s)
        softmax_pv(kt + 1, sb_s)
        return carry

    lax.fori_loop(0, (n_tiles + 1) // 2, slc_pair, 0)
    o_slc = acc_s[...] / l_s[...]

    outs = []
    for g in range(G):
        base = SMALL_G + (kh * G + g) * 3
        sl = slice(g * QB, (g + 1) * QB)
        o = (gate_s[pl.ds(base, 1), :] * o_cmp[:, sl] + gate_s[pl.ds(base + 1, 1), :] * o_slc[:, sl]
             + gate_s[pl.ds(base + 2, 1), :] * o_win[:, sl])
        outs.append(o.T)
    y_ref[...] = jnp.concatenate(outs, axis=1).astype(BF16)


def _nsa_attn(proj, cos_f, sin_f, kc, vct, ks, vst, kw, vwt):
    s = proj.shape[0]
    Hk, Dh, QB, G = N_KV_HEADS, N_HEAD_DIM, Q_BLOCK, N_REP
    n_cmp = s // CMP_STRIDE
    nw = WINDOW // QB + 1
    ratio = SLC_BLOCK // CMP_STRIDE
    row = np.arange(n_cmp)
    j_cmp = (row % (n_cmp // ratio)) * ratio + row // (n_cmp // ratio)
    cmp_end = jnp.asarray((j_cmp * CMP_STRIDE + (CMP_BLOCK - 1))[:, None] - np.arange(QB)[None, :], jnp.int32)
    head = lambda a, b: pl.BlockSpec((None, a, b), lambda kh, qb: (kh, 0, 0))
    tab = pl.BlockSpec((QB, Dh), lambda kh, qb: (qb, 0))
    kw_specs = [pl.BlockSpec((None, QB, Dh), functools.partial(
        lambda kh, qb, w: (kh, jnp.maximum(qb - (nw - 1) + w, 0), 0), w=w)) for w in range(nw)]
    vw_specs = [pl.BlockSpec((None, Dh, QB), functools.partial(
        lambda kh, qb, w: (kh, 0, jnp.maximum(qb - (nw - 1) + w, 0)), w=w)) for w in range(nw)]
    return pl.pallas_call(
        functools.partial(_nsa_attn_kernel, seq=s),
        grid=(Hk, s // QB),
        in_specs=[pl.BlockSpec((QB, G * Dh), lambda kh, qb: (qb, COL_NQ // (G * Dh) + kh)),
                  pl.BlockSpec((QB, 128), lambda kh, qb: (qb, COL_SMALL // 128)),
                  tab, tab, pl.BlockSpec((n_cmp, QB), lambda kh, qb: (0, 0)),
                  head(n_cmp, Dh), head(Dh, n_cmp), head(s, 2 * Dh), head(Dh, s),
                  pl.BlockSpec((None, QB, 2 * Dh), lambda kh, qb: (kh, qb, 0)),
                  pl.BlockSpec((None, Dh, QB), lambda kh, qb: (kh, 0, qb))] + kw_specs + vw_specs,
        out_specs=pl.BlockSpec((QB, G * Dh), lambda kh, qb: (qb, kh)),
        out_shape=jax.ShapeDtypeStruct((s, N_WIDTH), BF16),
        scratch_shapes=[pltpu.VMEM((128, QB), F32),
                        pltpu.VMEM((s // SLC_BLOCK, QB), F32),
                        pltpu.VMEM((2 * Dh, G * QB), BF16),
                        pltpu.VMEM((SLC_TILE, G * QB), F32),
                        pltpu.VMEM((SLC_TILE, G * QB), F32),
                        pltpu.VMEM((1, G * QB), F32),
                        pltpu.VMEM((1, G * QB), F32),
                        pltpu.VMEM((Dh, G * QB), F32)],
        compiler_params=_params("arbitrary", "arbitrary"),
    )(proj, proj, cos_f, sin_f, cmp_end, kc, vct, ks, vst, ks, vst, *([kw] * nw), *([vwt] * nw))


def _slab_major(a):
    hk, n, dh = a.shape
    ratio = SLC_BLOCK // CMP_STRIDE
    return a.reshape(hk, n // ratio, ratio, dh).transpose(0, 2, 1, 3).reshape(hk, n, dh)


def _nsa(proj, cos_f, sin_f, cmp_pos, cmp_w1, cmp_w2):
    kcmp, vcmp, ks, vst, kw, vwt = _nsa_pre(proj, cos_f, sin_f)
    kc = _slab_major(_compress(kcmp, cmp_pos[0], cmp_w1[0], cmp_w2[0])).astype(BF16)
    vc = _slab_major(_compress(vcmp, cmp_pos[1], cmp_w1[1], cmp_w2[1])).astype(BF16)
    return _nsa_attn(proj, cos_f, sin_f, kc, jnp.swapaxes(vc, 1, 2), ks, vst, kw, vwt)


def _outproj_kernel(ym_ref, ys_ref, yn_ref, w_ref, x_ref, g_ref, gt_ref, o_ref):
    y = (_dot(ym_ref[...], w_ref[0:M_WIDTH, :]) + _dot(ys_ref[...], w_ref[M_WIDTH:M_WIDTH + S_WIDTH, :])
         + _dot(yn_ref[...], w_ref[M_WIDTH + S_WIDTH:, :]))
    o_ref[...] = x_ref[...] + gt_ref[...] * ((y * _rms(y)) * g_ref[...])


def _outproj(ym, ys, yn, w, x, g, gt):
    s, d = x.shape
    tm = 512
    row = pl.BlockSpec((1, d), lambda i: (0, 0))
    rows = lambda width: pl.BlockSpec((tm, width), lambda i: (i, 0))
    return pl.pallas_call(
        _outproj_kernel,
        grid=(s // tm,),
        in_specs=[rows(M_WIDTH), rows(S_WIDTH), rows(N_WIDTH), pl.BlockSpec((d, d), lambda i: (0, 0)),
                  rows(d), row, row],
        out_specs=rows(d),
        out_shape=jax.ShapeDtypeStruct((s, d), F32),
        compiler_params=_params("arbitrary"),
    )(ym, ys, yn, w, x, g, gt)


def _ffn_kernel(x_ref, g_ref, sc_ref, sh_ref, wa_ref, wu_ref, cw_ref, cb_ref, wd_ref, gp_ref, gt_ref,
                o_ref, h_ref, acc_ref, carry_ref):
    i = pl.program_id(0)
    j = pl.program_id(1)
    tm = x_ref.shape[0]

    @pl.when(j == 0)
    def _():
        x = x_ref[...]
        h = (x * _rms(x)) * g_ref[...]
        h_ref[...] = (h * (1.0 + sc_ref[...]) + sh_ref[...]).astype(BF16)
        acc_ref[...] = jnp.zeros_like(acc_ref)

    @pl.when(i == 0)
    def _():
        carry_ref[j] = jnp.zeros(carry_ref.shape[1:], F32)

    h = h_ref[...]
    a = _dot(h, wa_ref[...])
    u = _dot(h, wu_ref[...])
    prev = carry_ref[j]
    carry_ref[j] = a[tm - 8:tm, :]
    row = lax.broadcasted_iota(jnp.int32, a.shape, 0)
    a1 = jnp.where(row == 0, prev[7:8, :], pltpu.roll(a, 1, axis=0))
    a2 = jnp.where(row == 0, prev[6:7, :], jnp.where(row == 1, prev[7:8, :], pltpu.roll(a, 2, axis=0)))
    cw = cw_ref[...]
    conv = cw[0:1] * a2 + cw[1:2] * a1 + cw[2:3] * a + cb_ref[...]
    acc_ref[...] += _dot((_gelu(conv) * u).astype(BF16), wd_ref[...])

    @pl.when(j == pl.num_programs(1) - 1)
    def _():
        y = acc_ref[...]
        o_ref[...] = x_ref[...] + gt_ref[...] * ((y * _rms(y)) * gp_ref[...])


def _ffn(x, g, sc, sh, wa, wu, cw, cb, wd, gp, gt):
    s, d = x.shape
    fp = wa.shape[1]
    tm, tf = 512, FFN_TILE
    nj = fp // tf
    row = pl.BlockSpec((1, d), lambda i, j: (0, 0))
    return pl.pallas_call(
        _ffn_kernel,
        grid=(s // tm, nj),
        in_specs=[pl.BlockSpec((tm, d), lambda i, j: (i, 0)), row, row, row,
                  pl.BlockSpec((d, tf), lambda i, j: (0, j)),
                  pl.BlockSpec((d, tf), lambda i, j: (0, j)),
                  pl.BlockSpec((3, tf), lambda i, j: (0, j)),
                  pl.BlockSpec((1, tf), lambda i, j: (0, j)),
                  pl.BlockSpec((tf, d), lambda i, j: (j, 0)),
                  row, row],
        out_specs=pl.BlockSpec((tm, d), lambda i, j: (i, 0)),
        out_shape=jax.ShapeDtypeStruct((s, d), F32),
        scratch_shapes=[pltpu.VMEM((tm, d), BF16), pltpu.VMEM((tm, d), F32), pltpu.VMEM((nj, 8, tf), F32)],
        compiler_params=_params("arbitrary", "arbitrary"),
    )(x, g, sc, sh, wa, wu, cw, cb, wd, gp, gt)


def _proj_weight(w_in):
    d = w_in.shape[0]
    offs = np.cumsum([0, 2 * M_WIDTH, M_WIDTH, M_WIDTH, M_HEADS, M_HEADS, S_WIDTH, N_WIDTH,
                      6 * N_KV_HEADS * N_HEAD_DIM, 3 * N_HEADS])
    seg = [w_in[:, offs[k]:offs[k + 1]] for k in range(9)]
    m_qk, m_v, m_o, m_i, m_f, s_u, n_q, n_kv, n_g = seg
    small = jnp.concatenate([m_i, m_f, n_g], axis=1)
    pad = jnp.zeros((d, N_PROJ_PAD - COL_SMALL - small.shape[1]), w_in.dtype)
    return jnp.concatenate([m_qk, m_v, m_o, s_u, n_q, n_kv, small, pad], axis=1).astype(BF16)


def _pad_cols(a, n):
    return jnp.concatenate([a, jnp.zeros(a.shape[:-1] + (n - a.shape[-1],), a.dtype)], axis=-1)


def kernel(x, c, positions, w_ada, b_ada, g_pre_mix, g_post_mix, g_pre_ffn, g_post_ffn, w_in, w_out, m_conv_w, m_conv_b, m_i_bias, m_f_bias, m_norm_g, s_lam_re, s_lam_im, s_log_dt, s_b_re, s_b_im, s_c_re, s_c_im, s_d, s_w_glu, s_b_glu, n_cmp_pos, n_cmp_w1, n_cmp_w2, f_w_up, f_conv_w, f_conv_b, f_w_down):
    batch, seq, d = x.shape
    assert batch == 1 and d == D_MODEL and seq % 1024 == 0
    depth = w_ada.shape[0]
    xs = x.reshape(seq, d)
    mod = _ada_mod(c.reshape(d, 1), w_ada, b_ada)
    cos_f, sin_f = _rope_tables(positions.reshape(seq, 1))
    row = lambda a: a.reshape(1, -1)
    for l in range(depth):
        sh1, sc1, gt1, sh2, sc2, gt2 = [mod[l, :, k * d:(k + 1) * d] for k in range(6)]
        proj = _inproj(xs, row(g_pre_mix[l]), sc1, sh1, _proj_weight(w_in[l]))
        gate_bias = _pad_cols(jnp.concatenate([m_i_bias[l], m_f_bias[l]]).reshape(1, -1), 128)
        y_m = _mlstm(proj, m_conv_w[l], row(m_conv_b[l]), gate_bias, row(m_norm_g[l]))
        ops = _s5_prep(s_lam_re[l], s_lam_im[l], s_log_dt[l], s_b_re[l], s_b_im[l], s_c_re[l], s_c_im[l])
        y_s = _s5(proj, ops, row(s_d[l]), s_w_glu[l].astype(BF16), row(s_b_glu[l]))
        y_n = _nsa(proj, cos_f, sin_f, n_cmp_pos[l], n_cmp_w1[l], n_cmp_w2[l])
        xs = _outproj(y_m, y_s, y_n, w_out[l].astype(BF16), xs, row(g_post_mix[l]), gt1)
        wa = _pad_cols(f_w_up[l][:, :D_FF], D_FF_PAD).astype(BF16)
        wu = _pad_cols(f_w_up[l][:, D_FF:], D_FF_PAD).astype(BF16)
        wd = jnp.concatenate([f_w_down[l], jnp.zeros((D_FF_PAD - D_FF, d), F32)], axis=0).astype(BF16)
        xs = _ffn(xs, row(g_pre_ffn[l]), sc2, sh2, wa, wu, _pad_cols(f_conv_w[l], D_FF_PAD),
                  _pad_cols(row(f_conv_b[l]), D_FF_PAD), wd, row(g_post_ffn[l]), gt2)
    return xs.reshape(batch, seq, d)
```

```python
import functools

import numpy as np
import jax
import jax.numpy as jnp
from jax import lax
from jax.experimental import pallas as pl
from jax.experimental.pallas import tpu as pltpu

F32 = jnp.float32
BF16 = jnp.bfloat16

D_MODEL = 2048
EPS = 1e-6
M_WIDTH = 512
M_HEADS = 4
M_HEAD_DIM = 128
M_CHUNK = 128
M_CONV = 4
S_WIDTH = 512
S_GROUP = 16
S_GROUPS = 32
S_STATE = 64
S_T = 16
S_OCT = 8
N_WIDTH = 1024
N_HEAD_DIM = 128
N_HEADS = 8
N_KV_HEADS = 2
N_REP = N_HEADS // N_KV_HEADS
CMP_BLOCK = 32
CMP_STRIDE = 16
SLC_BLOCK = 64
SLC_TOPK = 16
WINDOW = 512
Q_BLOCK = 128
ROPE_THETA = 10000.0
NEG_INF = -1e30
FORCED_SCORE = 1e9
SLC_TILE = 512
V_EXTRA = 16
D_FF = 5504
D_FF_PAD = 5632
FFN_TILE = 512
COL_MQK, COL_MV, COL_MO, COL_SU, COL_NQ, COL_NKV, COL_SMALL = 0, 1024, 1536, 2048, 2560, 3584, 5120
N_PROJ_PAD = 5376
PROJ_TILE = 768
SMALL_I, SMALL_F, SMALL_G = 0, 4, 8

VMEM_LIMIT_BYTES = 58 * 1024 * 1024


def _params(*sem):
    return pltpu.CompilerParams(dimension_semantics=sem, vmem_limit_bytes=VMEM_LIMIT_BYTES)


def _gelu(x):
    return 0.5 * x * (1.0 + jnp.tanh(0.7978845608028654 * (x + 0.044715 * (x * x * x))))


def _rms(x):
    return lax.rsqrt(jnp.mean(x * x, axis=-1, keepdims=True) + EPS)


def _dot(a, b):
    return jnp.dot(a, b, preferred_element_type=F32)


def _dot_nt(a, b):
    return lax.dot_general(a, b, (((1,), (1,)), ((), ())), preferred_element_type=F32)


def _ada_kernel(c_ref, w_ref, b_ref, o_ref):
    c = c_ref[...]
    cs = c * jax.nn.sigmoid(c)
    o_ref[...] = jnp.sum(w_ref[...] * cs, axis=0, keepdims=True) + b_ref[...]


def _ada_mod(c_col, w_ada, b_ada):
    depth, d, n = w_ada.shape
    tn = 1024
    return pl.pallas_call(
        _ada_kernel,
        grid=(depth, n // tn),
        in_specs=[pl.BlockSpec((d, 1), lambda l, j: (0, 0)),
                  pl.BlockSpec((None, d, tn), lambda l, j: (l, 0, j)),
                  pl.BlockSpec((None, 1, tn), lambda l, j: (l, 0, j))],
        out_specs=pl.BlockSpec((None, 1, tn), lambda l, j: (l, 0, j)),
        out_shape=jax.ShapeDtypeStruct((depth, 1, n), F32),
        compiler_params=_params("arbitrary", "arbitrary"),
    )(c_col, w_ada, b_ada.reshape(depth, 1, n))


def _inproj_kernel(x_ref, g_ref, sc_ref, sh_ref, w_ref, o_ref, h_ref):
    @pl.when(pl.program_id(1) == 0)
    def _():
        x = x_ref[...]
        h = (x * _rms(x)) * g_ref[...]
        h_ref[...] = (h * (1.0 + sc_ref[...]) + sh_ref[...]).astype(BF16)

    o_ref[...] = _dot(h_ref[...], w_ref[...])


def _inproj(x, g, sc, sh, w):
    s, d = x.shape
    n = w.shape[1]
    tm, tn = min(s, 1024), PROJ_TILE
    row = pl.BlockSpec((1, d), lambda i, j: (0, 0))
    return pl.pallas_call(
        _inproj_kernel,
        grid=(s // tm, n // tn),
        in_specs=[pl.BlockSpec((tm, d), lambda i, j: (i, 0)), row, row, row,
                  pl.BlockSpec((d, tn), lambda i, j: (0, j))],
        out_specs=pl.BlockSpec((tm, tn), lambda i, j: (i, j)),
        out_shape=jax.ShapeDtypeStruct((s, n), F32),
        scratch_shapes=[pltpu.VMEM((tm, d), BF16)],
        compiler_params=_params("arbitrary", "arbitrary"),
    )(x, g, sc, sh, w)


def _mlstm_kernel(qk_ref, v_ref, o_ref, sm_ref, cw_ref, cb_ref, gb_ref, ng_ref, y_ref,
                  tail_ref, c_ref, n_ref, m_ref):
    L, Dh = M_CHUNK, M_HEAD_DIM

    @pl.when(pl.program_id(0) == 0)
    def _():
        tail_ref[...] = jnp.zeros_like(tail_ref)
        c_ref[...] = jnp.zeros_like(c_ref)
        n_ref[...] = jnp.zeros_like(n_ref)
        m_ref[...] = jnp.zeros_like(m_ref)

    raw = qk_ref[...]
    tail = tail_ref[...]
    row = lax.broadcasted_iota(jnp.int32, raw.shape, 0)

    def shifted(d):
        r = pltpu.roll(raw, d, axis=0)
        for k in range(d):
            r = jnp.where(row == k, tail[8 - d + k:8 - d + k + 1, :], r)
        return r

    cw = cw_ref[...]
    pre = (cw[3:4] * raw + cw[2:3] * shifted(1) + cw[1:2] * shifted(2) + cw[0:1] * shifted(3)
           + cb_ref[...])
    tail_ref[...] = raw[L - 8:L, :]
    qk = pre * jax.nn.sigmoid(pre)

    sm = sm_ref[...] + gb_ref[...]
    lf = jnp.minimum(sm, 0.0) - jnp.log1p(jnp.exp(-jnp.abs(sm)))
    rowg = lax.broadcasted_iota(jnp.int32, lf.shape, 0)
    g = lf
    for s in (1, 2, 4, 8, 16, 32, 64):
        g = g + jnp.where(rowg >= s, pltpu.roll(g, s, axis=0), 0.0)
    smT = sm.T
    gT = g.T
    ri = lax.broadcasted_iota(jnp.int32, (L, L), 0)
    ci = lax.broadcasted_iota(jnp.int32, (L, L), 1)
    causal = ri >= ci

    for h in range(M_HEADS):
        hs = slice(h * Dh, (h + 1) * Dh)
        q = qk[:, h * Dh:(h + 1) * Dh]
        k = qk[:, M_WIDTH + h * Dh:M_WIDTH + (h + 1) * Dh] * (Dh ** -0.5)
        v = v_ref[:, hs]
        i_c = sm[:, SMALL_I + h:SMALL_I + h + 1]
        g_c = g[:, SMALL_F + h:SMALL_F + h + 1]
        i_r = smT[SMALL_I + h:SMALL_I + h + 1, :]
        g_r = gT[SMALL_F + h:SMALL_F + h + 1, :]
        g_last = g_r[:, L - 1:L]
        a_r = g_last - g_r + i_r
        a_max = jnp.max(a_r, axis=1, keepdims=True)
        w_c = jnp.exp(g_last - g_c + i_c - a_max)
        m_prev = m_ref[h:h + 1, 0:1]
        c_prev = c_ref[h]
        n_prev = n_ref[h:h + 1, :]
        qb = q.astype(BF16)
        kb = k.astype(BF16)
        vb = v.astype(BF16)
        c_loc = _dot(k.T.astype(BF16), (w_c * v).astype(BF16))
        n_loc = jnp.sum(w_c * k, axis=0, keepdims=True)
        m_new = jnp.maximum(g_last + m_prev, a_max)
        s_old = jnp.exp(g_last + m_prev - m_new)
        s_new = jnp.exp(a_max - m_new)
        c_ref[h] = s_old * c_prev + s_new * c_loc
        n_ref[h:h + 1, :] = s_old * n_prev + s_new * n_loc
        m_ref[h:h + 1, :] = jnp.broadcast_to(m_new, (1, Dh))
        dlog = jnp.where(causal, g_c - g_r + i_r, -jnp.inf)
        m_inter = g_c + m_prev
        m_t = jnp.maximum(m_inter, jnp.max(dlog, axis=1, keepdims=True))
        sw = _dot_nt(qb, kb) * jnp.exp(dlog - m_t)
        inter = jnp.exp(m_inter - m_t)
        num = _dot(sw.astype(BF16), vb) + inter * _dot(qb, c_prev.astype(BF16))
        den = jnp.sum(sw, axis=1, keepdims=True) + inter * jnp.sum(q * n_prev, axis=1, keepdims=True)
        hh = num / jnp.maximum(jnp.abs(den), jnp.exp(-m_t))
        hn = (hh * _rms(hh)) * ng_ref[:, hs]
        y_ref[:, hs] = (jax.nn.sigmoid(o_ref[:, hs]) * hn).astype(BF16)


def _mlstm(proj, conv_w, conv_b, gate_bias, norm_g):
    s = proj.shape[0]
    L = M_CHUNK
    full = lambda shape: pl.BlockSpec(shape, lambda c: (0,) * len(shape))
    return pl.pallas_call(
        _mlstm_kernel,
        grid=(s // L,),
        in_specs=[pl.BlockSpec((L, 2 * M_WIDTH), lambda c: (c, COL_MQK // (2 * M_WIDTH))),
                  pl.BlockSpec((L, M_WIDTH), lambda c: (c, COL_MV // M_WIDTH)),
                  pl.BlockSpec((L, M_WIDTH), lambda c: (c, COL_MO // M_WIDTH)),
                  pl.BlockSpec((L, 128), lambda c: (c, COL_SMALL // 128)),
                  full((M_CONV, 2 * M_WIDTH)), full((1, 2 * M_WIDTH)), full((1, 128)), full((1, M_WIDTH))],
        out_specs=pl.BlockSpec((L, M_WIDTH), lambda c: (c, 0)),
        out_shape=jax.ShapeDtypeStruct((s, M_WIDTH), BF16),
        scratch_shapes=[pltpu.VMEM((8, 2 * M_WIDTH), F32),
                        pltpu.VMEM((M_HEADS, M_HEAD_DIM, M_HEAD_DIM), F32),
                        pltpu.VMEM((8, M_HEAD_DIM), F32),
                        pltpu.VMEM((8, M_HEAD_DIM), F32)],
        compiler_params=_params("arbitrary"),
    )(proj, proj, proj, proj, conv_w, conv_b, gate_bias, norm_g)


def _split3(a):
    hi = a.astype(BF16)
    r = a - hi.astype(F32)
    mid = r.astype(BF16)
    lo = (r - mid.astype(F32)).astype(BF16)
    return hi, mid, lo


def _dot_f32(a, b):
    ah, am, al = _split3(a)
    bh, bm, bl = _split3(b)
    small = _dot(ah, bl) + _dot(am, bm) + _dot(al, bh)
    mid = _dot(ah, bm) + _dot(am, bh)
    return _dot(ah, bh) + (mid + small)


def _s5_prep_kernel(lrr_ref, lir_ref, lrc_ref, lic_ref, ldt_ref, btr_ref, bti_ref, ctr_ref, cti_ref, place_ref,
                    m_ref, pr_ref, pi_ref, rr_ref, ri_ref, ar_ref, ai_ref):
    T = S_T
    dt = jnp.exp(ldt_ref[...])
    lr, li = lrr_ref[...], lir_ref[...]
    mag = jnp.exp(lr * dt)
    a_re = mag * jnp.cos(li * dt)
    a_im = mag * jnp.sin(li * dt)
    den = lr * lr + li * li
    nr = a_re - 1.0
    z_re = (nr * lr + a_im * li) / den
    z_im = (a_im * lr - nr * li) / den
    btr, bti = btr_ref[...], bti_ref[...]
    bb_re = z_re * btr - z_im * bti
    bb_im = z_re * bti + z_im * btr

    def pow_row(e):
        mg = jnp.exp((lr * dt) * e)
        th = (li * dt) * e
        return mg * jnp.cos(th), mg * jnp.sin(th)

    jrow = lax.shift_right_logical(lax.broadcasted_iota(jnp.int32, btr.shape, 0), 4).astype(F32)
    p_re, p_im = pow_row(-jrow)
    left_re = bb_re * p_re - bb_im * p_im
    left_im = bb_re * p_im + bb_im * p_re
    q_re, q_im = pow_row((T - 1.0) - jrow)
    pr_ref[...] = bb_re * q_re - bb_im * q_im
    pi_ref[...] = bb_re * q_im + bb_im * q_re

    lrc, lic = lrc_ref[...], lic_ref[...]
    ctr, cti = ctr_ref[...], cti_ref[...]

    def pow_col(e):
        mg = jnp.exp((lrc * dt) * e)
        th = (lic * dt) * e
        return mg * jnp.cos(th), mg * jnp.sin(th)

    icol = lax.shift_right_logical(lax.broadcasted_iota(jnp.int32, ctr.shape, 1), 4).astype(F32)
    e_re, e_im = pow_col(icol)
    right_re = ctr * e_re - cti * e_im
    right_im = ctr * e_im + cti * e_re
    m = _dot_f32(left_re, right_re) - _dot_f32(left_im, right_im)
    mj = lax.shift_right_logical(lax.broadcasted_iota(jnp.int32, m.shape, 0), 4)
    mi = lax.shift_right_logical(lax.broadcasted_iota(jnp.int32, m.shape, 1), 4)
    m_ref[...] = _dot(jnp.where(mi >= mj, m, 0.0).astype(BF16), place_ref[...]).astype(BF16)
    f_re, f_im = pow_col(icol + 1.0)
    rr_ref[...] = ctr * f_re - cti * f_im
    ri_ref[...] = -(ctr * f_im + cti * f_re)
    mag_t = jnp.exp((lr * dt) * T)
    ar_ref[...] = mag_t * jnp.cos((li * dt) * T)
    ai_ref[...] = mag_t * jnp.sin((li * dt) * T)


def _s5_prep(lam_re, lam_im, log_dt, b_re, b_im, c_re, c_im):
    G, N, C, T = S_GROUPS, S_STATE, S_GROUP, S_T
    TC = T * C
    bt = lambda b: jnp.tile(jnp.swapaxes(b, 1, 2), (1, T, 1))
    ct = lambda c: jnp.tile(jnp.swapaxes(c, 1, 2), (1, 1, T))
    blk = lambda a, b: pl.BlockSpec((None, a, b), lambda g: (g, 0, 0))
    O, NO = S_OCT, G // S_OCT
    W = T * O * C
    col = np.arange(W)
    src = (col // (O * C)) * C + col % C
    place = np.stack([(np.arange(TC)[:, None] == src[None, :]) & ((col // C) % O == g)[None, :]
                      for g in range(O)]).astype(np.float32)
    outs = pl.pallas_call(
        _s5_prep_kernel,
        grid=(G,),
        in_specs=[blk(1, N), blk(1, N), blk(N, 1), blk(N, 1), blk(1, 1),
                  blk(TC, N), blk(TC, N), blk(N, TC), blk(N, TC),
                  pl.BlockSpec((None, TC, W), lambda g: (g % O, 0, 0))],
        out_specs=[blk(TC, W), blk(TC, N), blk(TC, N), blk(N, TC), blk(N, TC), blk(1, N), blk(1, N)],
        out_shape=[jax.ShapeDtypeStruct((G, TC, W), BF16),
                   jax.ShapeDtypeStruct((G, TC, N), F32), jax.ShapeDtypeStruct((G, TC, N), F32),
                   jax.ShapeDtypeStruct((G, N, TC), F32), jax.ShapeDtypeStruct((G, N, TC), F32),
                   jax.ShapeDtypeStruct((G, 1, N), F32), jax.ShapeDtypeStruct((G, 1, N), F32)],
        compiler_params=_params("arbitrary"),
    )(lam_re.reshape(G, 1, N), lam_im.reshape(G, 1, N), lam_re.reshape(G, N, 1), lam_im.reshape(G, N, 1),
      log_dt.reshape(G, 1, 1), bt(b_re), bt(b_im), ct(c_re), ct(c_im), jnp.asarray(place, BF16))
    m, pr, pi, rr, ri, ar, ai = outs

    eye = jnp.eye(O, dtype=F32)
    m_oct = m.reshape(NO, O, T, C, W).transpose(0, 2, 1, 3, 4).reshape(NO, W, W)
    p_oct = lambda p: (p.reshape(NO, O, T, C, 1, N) * eye.reshape(1, O, 1, 1, O, 1)
                       ).transpose(0, 2, 1, 3, 4, 5).reshape(NO, T * O * C, O * N).astype(BF16)
    r_oct = lambda r: (r.reshape(NO, O, N, T, 1, C) * eye.reshape(1, O, 1, 1, O, 1)
                       ).reshape(NO, O * N, T * O * C).astype(BF16)
    oct_row = lambda a: a.reshape(NO, 1, O * N)
    return (m_oct, p_oct(pr), p_oct(pi), r_oct(rr), r_oct(ri), oct_row(ar), oct_row(ai))


def _s5_kernel(u_ref, m_ref, pr_ref, pi_ref, rr_ref, ri_ref, ar_ref, ai_ref, y_ref,
               xr_s, xi_s, er_s, ei_s, st_s):
    T = S_T
    nch = u_ref.shape[0] // T

    @pl.when(pl.program_id(1) == 0)
    def _():
        st_s[...] = jnp.zeros_like(st_s)

    u = jnp.concatenate([u_ref[pl.ds(j, nch, stride=T), :] for j in range(T)], axis=1).astype(BF16)
    y_loc = _dot(u, m_ref[...])
    xr_s[...] = _dot(u, pr_ref[...])
    xi_s[...] = _dot(u, pi_ref[...])
    a_re, a_im = ar_ref[...], ai_ref[...]

    def body(b, carry):
        e_re, e_im = carry
        base = pl.multiple_of(b * 8, 8)
        xr = xr_s[pl.ds(base, 8), :]
        xi = xi_s[pl.ds(base, 8), :]
        for k in range(8):
            er_s[pl.ds(base + k, 1), :] = e_re
            ei_s[pl.ds(base + k, 1), :] = e_im
            n_re = a_re * e_re - a_im * e_im + xr[k:k + 1]
            n_im = a_re * e_im + a_im * e_re + xi[k:k + 1]
            e_re, e_im = n_re, n_im
        return e_re, e_im

    e_re, e_im = lax.fori_loop(0, nch // 8, body, (st_s[0:1, :], st_s[1:2, :]))
    st_s[0:1, :] = e_re
    st_s[1:2, :] = e_im
    y = (y_loc + _dot(er_s[...].astype(BF16), rr_ref[...]) + _dot(ei_s[...].astype(BF16), ri_ref[...]))
    lanes = y_ref.shape[1]
    for i in range(T):
        y_ref[pl.ds(i, nch, stride=T), :] = y[:, i * lanes:(i + 1) * lanes]


def _s5_scan(proj, ops):
    m, pr, pi, rr, ri, ar, ai = ops
    s = proj.shape[0]
    no, w, ns = pr.shape
    lanes = S_OCT * S_GROUP
    ts = min(s, 4096)
    nch = ts // S_T
    op = lambda a, b: pl.BlockSpec((None, a, b), lambda o, r: (o, 0, 0))
    return pl.pallas_call(
        _s5_kernel,
        grid=(no, s // ts),
        in_specs=[pl.BlockSpec((ts, lanes), lambda o, r: (r, COL_SU // lanes + o)),
                  op(w, w), op(w, ns), op(w, ns), op(ns, w), op(ns, w), op(1, ns), op(1, ns)],
        out_specs=pl.BlockSpec((None, ts, lanes), lambda o, r: (o, r, 0)),
        out_shape=jax.ShapeDtypeStruct((no, s, lanes), F32),
        scratch_shapes=[pltpu.VMEM((nch, ns), F32) for _ in range(4)] + [pltpu.VMEM((8, ns), F32)],
        compiler_params=_params("arbitrary", "arbitrary"),
    )(proj, m, pr, pi, rr, ri, ar, ai)


def _s5_glu_kernel(y_ref, u_ref, d_ref, w_ref, b_ref, o_ref):
    y = jnp.concatenate([y_ref[o] for o in range(y_ref.shape[0])], axis=1)
    y = _gelu(y + d_ref[...] * u_ref[...])
    z = _dot(y.astype(BF16), w_ref[...]) + b_ref[...]
    o_ref[...] = (y * jax.nn.sigmoid(z)).astype(BF16)


def _s5_glu(y, proj, d_skip, w_glu, b_glu):
    no, s, lanes = y.shape
    w = no * lanes
    tm = 1024
    row = pl.BlockSpec((1, w), lambda i: (0, 0))
    return pl.pallas_call(
        _s5_glu_kernel,
        grid=(s // tm,),
        in_specs=[pl.BlockSpec((no, tm, lanes), lambda i: (0, i, 0)),
                  pl.BlockSpec((tm, w), lambda i: (i, COL_SU // S_WIDTH)),
                  row, pl.BlockSpec((w, w), lambda i: (0, 0)), row],
        out_specs=pl.BlockSpec((tm, w), lambda i: (i, 0)),
        out_shape=jax.ShapeDtypeStruct((s, w), BF16),
        compiler_params=_params("arbitrary"),
    )(y, proj, d_skip, w_glu, b_glu)


def _s5(proj, ops, d_skip, w_glu, b_glu):
    return _s5_glu(_s5_scan(proj, ops), proj, d_skip, w_glu, b_glu)


def _rope_tab_kernel(pos_ref, inv_ref, sgn_ref, cos_ref, sin_ref):
    ang = pos_ref[...].astype(F32) * inv_ref[...]
    cos_ref[...] = jnp.cos(ang)
    sin_ref[...] = jnp.sin(ang) * sgn_ref[...]


def _rope_tables(pos_col):
    s = pos_col.shape[0]
    half = N_HEAD_DIM // 2
    inv = (ROPE_THETA ** (-jnp.arange(half, dtype=F32) / half))
    inv = jnp.concatenate([inv, inv]).reshape(1, N_HEAD_DIM)
    sgn = jnp.concatenate([-jnp.ones((half,), F32), jnp.ones((half,), F32)]).reshape(1, N_HEAD_DIM)
    tm = min(s, 2048)
    row = pl.BlockSpec((1, N_HEAD_DIM), lambda i: (0, 0))
    tab = pl.BlockSpec((tm, N_HEAD_DIM), lambda i: (i, 0))
    return pl.pallas_call(
        _rope_tab_kernel,
        grid=(s // tm,),
        in_specs=[pl.BlockSpec((tm, 1), lambda i: (i, 0)), row, row],
        out_specs=[tab, tab],
        out_shape=[jax.ShapeDtypeStruct((s, N_HEAD_DIM), F32)] * 2,
        compiler_params=_params("arbitrary"),
    )(pos_col, inv, sgn)


def _col_reduce(x, op):
    while x.shape[0] > 8 and x.shape[0] % 64 == 0:
        x = op(x.reshape(8, x.shape[0] // 8, x.shape[1]), axis=0)
    return op(x, axis=0, keepdims=True)


def _colmax(x):
    return _col_reduce(x, jnp.max)


def _colsum(x):
    return _col_reduce(x, jnp.sum)


def _rope(x, cos_f, sin_f):
    return x * cos_f + pltpu.roll(x, N_HEAD_DIM // 2, axis=1) * sin_f


def _nsa_pre_kernel(*refs):
    kv = refs[0:6 * N_KV_HEADS]
    cos_ref, sin_ref = refs[12], refs[13]
    kcmp_o, vcmp_o, kslc_o, vslct_o, kwin_o, vwint_o = refs[14:]
    cos_f, sin_f = cos_ref[...], sin_ref[...]
    tm = cos_f.shape[0]
    key = pl.program_id(0) * tm + lax.broadcasted_iota(jnp.int32, (tm, N_HEAD_DIM), 0)
    blk = lax.shift_right_logical(key, 6) & (SLC_TILE // SLC_BLOCK - 1)
    onehot = jnp.where(lax.broadcasted_iota(jnp.int32, (tm, N_HEAD_DIM), 1) == blk, 1.0, 0.0).astype(BF16)
    ones_rows = jnp.where(lax.broadcasted_iota(jnp.int32, (V_EXTRA, tm), 0) == 0, 1.0, 0.0)
    for kh in range(N_KV_HEADS):
        kcmp_o[kh] = _rope(kv[0 * N_KV_HEADS + kh][...], cos_f, sin_f)
        vcmp_o[kh] = kv[1 * N_KV_HEADS + kh][...]
        kslc_o[kh] = jnp.concatenate(
            [_rope(kv[2 * N_KV_HEADS + kh][...], cos_f, sin_f).astype(BF16), onehot], axis=1)
        vslct_o[kh] = jnp.concatenate([kv[3 * N_KV_HEADS + kh][...].T, ones_rows], axis=0).astype(BF16)
        kwin_o[kh] = _rope(kv[4 * N_KV_HEADS + kh][...], cos_f, sin_f).astype(BF16)
        vwint_o[kh] = jnp.concatenate([kv[5 * N_KV_HEADS + kh][...].T, ones_rows], axis=0).astype(BF16)


def _nsa_pre(proj, cos_f, sin_f):
    s = proj.shape[0]
    Hk, Dh = N_KV_HEADS, N_HEAD_DIM
    tm = 512
    kv_specs = [pl.BlockSpec((tm, Dh), functools.partial(lambda i, c: (i, c), c=COL_NKV // Dh + n))
                for n in range(6 * Hk)]
    tab = pl.BlockSpec((tm, Dh), lambda i: (i, 0))
    std = pl.BlockSpec((Hk, tm, Dh), lambda i: (0, i, 0))
    tr = pl.BlockSpec((Hk, Dh + V_EXTRA, tm), lambda i: (0, 0, i))
    ext = pl.BlockSpec((Hk, tm, 2 * Dh), lambda i: (0, i, 0))
    return pl.pallas_call(
        _nsa_pre_kernel,
        grid=(s // tm,),
        in_specs=kv_specs + [tab, tab],
        out_specs=[std, std, ext, tr, std, tr],
        out_shape=[jax.ShapeDtypeStruct((Hk, s, Dh), F32), jax.ShapeDtypeStruct((Hk, s, Dh), F32),
                   jax.ShapeDtypeStruct((Hk, s, 2 * Dh), BF16), jax.ShapeDtypeStruct((Hk, Dh + V_EXTRA, s), BF16),
                   jax.ShapeDtypeStruct((Hk, s, Dh), BF16), jax.ShapeDtypeStruct((Hk, Dh + V_EXTRA, s), BF16)],
        compiler_params=_params("arbitrary"),
    )(*([proj] * (6 * Hk)), cos_f, sin_f)


def _cmp_kernel(x_ref, p1_ref, p2_ref, wt_ref, wb_ref, w2_ref, o_ref):
    x = x_ref[...]
    first = _dot((x + p1_ref[...]).astype(BF16), wt_ref[...])
    second = _dot((x + p2_ref[...]).astype(BF16), wb_ref[...])
    npieces = x.shape[0]
    pre = first + pltpu.roll(second, npieces - 1, axis=0)
    o_ref[...] = _dot(_gelu(pre).astype(BF16), w2_ref[...])


def _compress(x, pos, w1, w2):
    hk, s, dh = x.shape
    npieces = s // CMP_STRIDE
    half = CMP_STRIDE * dh
    xp = x.reshape(hk, npieces, half)
    p1 = pos[:CMP_STRIDE].reshape(1, half)
    p2 = pos[CMP_STRIDE:].reshape(1, half)
    full = lambda shape: pl.BlockSpec(shape, lambda h: (0,) * len(shape))
    return pl.pallas_call(
        _cmp_kernel,
        grid=(hk,),
        in_specs=[pl.BlockSpec((None, npieces, half), lambda h: (h, 0, 0)),
                  full((1, half)), full((1, half)), full((half, dh)), full((half, dh)), full((dh, dh))],
        out_specs=pl.BlockSpec((None, npieces, dh), lambda h: (h, 0, 0)),
        out_shape=jax.ShapeDtypeStruct((hk, npieces, dh), F32),
        compiler_params=_params("arbitrary"),
    )(xp, p1, p2, w1[:half].astype(BF16), w1[half:].astype(BF16), w2.astype(BF16))


def _nsa_attn_kernel(*refs, seq):
    NW = WINDOW // Q_BLOCK + 1
    q_ref, sm_ref, cos_ref, sin_ref, cend_ref, kc_ref, vct_ref, ks_ref, vst_ref, ksd_ref, vsd_ref = refs[:11]
    kw_refs = refs[11:11 + NW]
    vw_refs = refs[11 + NW:11 + 2 * NW]
    y_ref, gate_s, bias_s, rhs_s, sa_s, sb_s, m_s, acc_s = refs[11 + 2 * NW:]
    Dh, QB, G, TK = N_HEAD_DIM, Q_BLOCK, N_REP, SLC_TILE
    n_slc = seq // SLC_BLOCK
    n_cmp = seq // CMP_STRIDE
    ratio = SLC_BLOCK // CMP_STRIDE
    kh = pl.program_id(0)
    qb = pl.program_id(1)
    t0 = qb * QB
    c2 = (Dh ** -0.5) * 1.4426950408889634
    cos_f, sin_f = cos_ref[...], sin_ref[...]
    q = q_ref[...]
    q_t = jnp.concatenate(
        [(_rope(q[:, g * Dh:(g + 1) * Dh], cos_f, sin_f) * c2).T.astype(BF16) for g in range(G)],
        axis=1)
    gate_s[...] = jax.nn.sigmoid(sm_ref[...]).T
    t_lane = t0 + lax.broadcasted_iota(jnp.int32, (1, QB), 1)

    s_all = _dot(kc_ref[...], q_t)
    valid = cend_ref[...] <= t0
    any_valid = t_lane >= CMP_BLOCK - 1
    pk = jnp.zeros((n_cmp, QB), F32)
    p_parts = []
    for g in range(G):
        s_g = jnp.where(valid, s_all[:, g * QB:(g + 1) * QB], NEG_INF)
        e = jnp.exp2(s_g - _colmax(s_g))
        p = e * jnp.where(any_valid, 1.0 / _colsum(e), 0.0)
        pk = pk + p
        p_parts.append(p.astype(BF16))
    o_cmp = _dot(vct_ref[...], jnp.concatenate(p_parts, axis=1))

    s = _dot(jnp.concatenate([r[...] for r in kw_refs], axis=0), q_t)
    wpos = t0 - WINDOW + lax.broadcasted_iota(jnp.int32, (WINDOW + QB, QB), 0)
    wbias = jnp.where(wpos <= t_lane, jnp.where(wpos > t_lane - WINDOW, jnp.where(wpos >= 0, 0.0, NEG_INF),
                                                NEG_INF), NEG_INF)
    s = s + jnp.concatenate([wbias] * G, axis=1)
    p = jnp.exp2(s - _colmax(s))
    o_win = _dot(jnp.concatenate([r[...] for r in vw_refs], axis=1), p.astype(BF16))
    o_win = o_win[0:Dh] / o_win[Dh:Dh + 1]

    rhs_s[0:Dh, :] = q_t
    rhs_s[Dh:2 * Dh, :] = jnp.zeros((Dh, G * QB), BF16)

    slabs = [pk[r * n_slc:(r + 1) * n_slc] for r in range(ratio)]
    row_a = lax.broadcasted_iota(jnp.int32, (n_slc, QB), 0)
    prev_last = jnp.where(row_a == 0, 0.0, pltpu.roll(slabs[ratio - 1], 1, axis=0))
    mid = slabs[0]
    for r in range(1, ratio - 1):
        mid = mid + slabs[r]
    imp = (prev_last + 2.0 * mid) + slabs[ratio - 1]
    tb = lax.shift_right_logical(t_lane, 6)
    allowed = row_a <= tb
    forced = (row_a == 0) | (row_a == tb) | (row_a == tb - 1)
    score = jnp.where(forced, -jnp.inf, jnp.where(allowed, imp, -jnp.inf))
    row_f = row_a.astype(F32)
    for _ in range(max(min(SLC_TOPK, n_slc) - 3, 0)):
        best = _colmax(score)
        idx = _col_reduce(jnp.where(score == best, row_f, float(n_slc)), jnp.min)
        score = jnp.where(row_f == idx, -jnp.inf, score)
    bias_s[...] = jnp.where(score == -jnp.inf, jnp.where(row_a < 2 * qb, 0.0, NEG_INF), NEG_INF)

    blocks_per_tile = TK // SLC_BLOCK
    n_tiles = (t0 + TK - 1) // TK

    def scores(kt, buf):
        live = kt < n_tiles
        kt = jnp.maximum(jnp.minimum(kt, n_tiles - 1), 0)
        b8 = bias_s[pl.ds(pl.multiple_of(kt * blocks_per_tile, blocks_per_tile), blocks_per_tile), :]
        b8 = jnp.where(live, b8, NEG_INF)
        rows = jnp.concatenate([jnp.concatenate([b8] * G, axis=1),
                                jnp.zeros((16 - blocks_per_tile, G * QB), F32)], axis=0)
        rhs_s[Dh:Dh + 16, :] = rows.astype(BF16)
        buf[...] = _dot(ks_ref[pl.ds(pl.multiple_of(kt * TK, TK), TK), :], rhs_s[...])

    def softmax_pv(kt, buf):
        kt = jnp.minimum(kt, n_tiles - 1)
        s = buf[...]
        m_old = m_s[...]
        m_new = jnp.maximum(m_old, _colmax(s))
        alpha = jnp.exp2(m_old - m_new)
        p = jnp.exp2(s - m_new)
        m_s[...] = m_new
        acc_s[...] = alpha * acc_s[...] + _dot(vst_ref[:, pl.ds(pl.multiple_of(kt * TK, TK), TK)], p.astype(BF16))

    scores(0, sa_s)

    s = _dot(ksd_ref[:, 0:Dh], q_t)
    kpos = t0 + lax.broadcasted_iota(jnp.int32, (QB, QB), 0)
    s = s + jnp.concatenate([jnp.where(kpos <= t_lane, 0.0, NEG_INF)] * G, axis=1)
    m0 = _colmax(s)
    p = jnp.exp2(s - m0)
    m_s[...] = m0
    acc_s[...] = _dot(vsd_ref[...], p.astype(BF16))

    def slc_pair(k2, carry):
        kt = 2 * k2
        scores(kt + 1, sb_s)
        softmax_pv(kt, sa_s)
        scores(kt + 2, sa_s)
        softmax_pv(kt + 1, sb_s)
        return carry

    lax.fori_loop(0, (n_tiles + 1) // 2, slc_pair, 0)
    o_slc = acc_s[0:Dh, :] / acc_s[Dh:Dh + 1, :]

    outs = []
    for g in range(G):
        base = SMALL_G + (kh * G + g) * 3
        sl = slice(g * QB, (g + 1) * QB)
        o = (gate_s[pl.ds(base, 1), :] * o_cmp[:, sl] + gate_s[pl.ds(base + 1, 1), :] * o_slc[:, sl]
             + gate_s[pl.ds(base + 2, 1), :] * o_win[:, sl])
        outs.append(o.T)
    y_ref[...] = jnp.concatenate(outs, axis=1).astype(BF16)


def _nsa_attn(proj, cos_f, sin_f, kc, vct, ks, vst, kw, vwt):
    s = proj.shape[0]
    Hk, Dh, QB, G = N_KV_HEADS, N_HEAD_DIM, Q_BLOCK, N_REP
    n_cmp = s // CMP_STRIDE
    nw = WINDOW // QB + 1
    ratio = SLC_BLOCK // CMP_STRIDE
    row = np.arange(n_cmp)
    j_cmp = (row % (n_cmp // ratio)) * ratio + row // (n_cmp // ratio)
    cmp_end = jnp.asarray((j_cmp * CMP_STRIDE + (CMP_BLOCK - 1))[:, None] - np.arange(QB)[None, :], jnp.int32)
    head = lambda a, b: pl.BlockSpec((None, a, b), lambda kh, qb: (kh, 0, 0))
    tab = pl.BlockSpec((QB, Dh), lambda kh, qb: (qb, 0))
    kw_specs = [pl.BlockSpec((None, QB, Dh), functools.partial(
        lambda kh, qb, w: (kh, jnp.maximum(qb - (nw - 1) + w, 0), 0), w=w)) for w in range(nw)]
    vw_specs = [pl.BlockSpec((None, Dh + V_EXTRA, QB), functools.partial(
        lambda kh, qb, w: (kh, 0, jnp.maximum(qb - (nw - 1) + w, 0)), w=w)) for w in range(nw)]
    return pl.pallas_call(
        functools.partial(_nsa_attn_kernel, seq=s),
        grid=(Hk, s // QB),
        in_specs=[pl.BlockSpec((QB, G * Dh), lambda kh, qb: (qb, COL_NQ // (G * Dh) + kh)),
                  pl.BlockSpec((QB, 128), lambda kh, qb: (qb, COL_SMALL // 128)),
                  tab, tab, pl.BlockSpec((n_cmp, QB), lambda kh, qb: (0, 0)),
                  head(n_cmp, Dh), head(Dh, n_cmp), head(s, 2 * Dh), head(Dh + V_EXTRA, s),
                  pl.BlockSpec((None, QB, 2 * Dh), lambda kh, qb: (kh, qb, 0)),
                  pl.BlockSpec((None, Dh + V_EXTRA, QB), lambda kh, qb: (kh, 0, qb))] + kw_specs + vw_specs,
        out_specs=pl.BlockSpec((QB, G * Dh), lambda kh, qb: (qb, kh)),
        out_shape=jax.ShapeDtypeStruct((s, N_WIDTH), BF16),
        scratch_shapes=[pltpu.VMEM((128, QB), F32),
                        pltpu.VMEM((s // SLC_BLOCK, QB), F32),
                        pltpu.VMEM((2 * Dh, G * QB), BF16),
                        pltpu.VMEM((SLC_TILE, G * QB), F32),
                        pltpu.VMEM((SLC_TILE, G * QB), F32),
                        pltpu.VMEM((1, G * QB), F32),
                        pltpu.VMEM((Dh + V_EXTRA, G * QB), F32)],
        compiler_params=_params("arbitrary", "arbitrary"),
    )(proj, proj, cos_f, sin_f, cmp_end, kc, vct, ks, vst, ks, vst, *([kw] * nw), *([vwt] * nw))


def _slab_major(a):
    hk, n, dh = a.shape
    ratio = SLC_BLOCK // CMP_STRIDE
    return a.reshape(hk, n // ratio, ratio, dh).transpose(0, 2, 1, 3).reshape(hk, n, dh)


def _nsa(proj, cos_f, sin_f, cmp_pos, cmp_w1, cmp_w2):
    kcmp, vcmp, ks, vst, kw, vwt = _nsa_pre(proj, cos_f, sin_f)
    kc = _slab_major(_compress(kcmp, cmp_pos[0], cmp_w1[0], cmp_w2[0])).astype(BF16)
    vc = _slab_major(_compress(vcmp, cmp_pos[1], cmp_w1[1], cmp_w2[1])).astype(BF16)
    return _nsa_attn(proj, cos_f, sin_f, kc, jnp.swapaxes(vc, 1, 2), ks, vst, kw, vwt)


def _outproj_kernel(ym_ref, ys_ref, yn_ref, w_ref, x_ref, g_ref, gt_ref, o_ref):
    y = (_dot(ym_ref[...], w_ref[0:M_WIDTH, :]) + _dot(ys_ref[...], w_ref[M_WIDTH:M_WIDTH + S_WIDTH, :])
         + _dot(yn_ref[...], w_ref[M_WIDTH + S_WIDTH:, :]))
    o_ref[...] = x_ref[...] + gt_ref[...] * ((y * _rms(y)) * g_ref[...])


def _outproj(ym, ys, yn, w, x, g, gt):
    s, d = x.shape
    tm = 512
    row = pl.BlockSpec((1, d), lambda i: (0, 0))
    rows = lambda width: pl.BlockSpec((tm, width), lambda i: (i, 0))
    return pl.pallas_call(
        _outproj_kernel,
        grid=(s // tm,),
        in_specs=[rows(M_WIDTH), rows(S_WIDTH), rows(N_WIDTH), pl.BlockSpec((d, d), lambda i: (0, 0)),
                  rows(d), row, row],
        out_specs=rows(d),
        out_shape=jax.ShapeDtypeStruct((s, d), F32),
        compiler_params=_params("arbitrary"),
    )(ym, ys, yn, w, x, g, gt)


def _ffn_kernel(x_ref, g_ref, sc_ref, sh_ref, wa_ref, wu_ref, cw_ref, cb_ref, wd_ref, gp_ref, gt_ref,
                o_ref, h_ref, acc_ref, carry_ref):
    i = pl.program_id(0)
    j = pl.program_id(1)
    tm = x_ref.shape[0]

    @pl.when(j == 0)
    def _():
        x = x_ref[...]
        h = (x * _rms(x)) * g_ref[...]
        h_ref[...] = (h * (1.0 + sc_ref[...]) + sh_ref[...]).astype(BF16)
        acc_ref[...] = jnp.zeros_like(acc_ref)

    @pl.when(i == 0)
    def _():
        carry_ref[j] = jnp.zeros(carry_ref.shape[1:], F32)

    h = h_ref[...]
    a = _dot(h, wa_ref[...])
    u = _dot(h, wu_ref[...])
    prev = carry_ref[j]
    carry_ref[j] = a[tm - 8:tm, :]
    row = lax.broadcasted_iota(jnp.int32, a.shape, 0)
    a1 = jnp.where(row == 0, prev[7:8, :], pltpu.roll(a, 1, axis=0))
    a2 = jnp.where(row == 0, prev[6:7, :], jnp.where(row == 1, prev[7:8, :], pltpu.roll(a, 2, axis=0)))
    cw = cw_ref[...]
    conv = cw[0:1] * a2 + cw[1:2] * a1 + cw[2:3] * a + cb_ref[...]
    acc_ref[...] += _dot((_gelu(conv) * u).astype(BF16), wd_ref[...])

    @pl.when(j == pl.num_programs(1) - 1)
    def _():
        y = acc_ref[...]
        o_ref[...] = x_ref[...] + gt_ref[...] * ((y * _rms(y)) * gp_ref[...])


def _ffn(x, g, sc, sh, wa, wu, cw, cb, wd, gp, gt):
    s, d = x.shape
    fp = wa.shape[1]
    tm, tf = 512, FFN_TILE
    nj = fp // tf
    row = pl.BlockSpec((1, d), lambda i, j: (0, 0))
    return pl.pallas_call(
        _ffn_kernel,
        grid=(s // tm, nj),
        in_specs=[pl.BlockSpec((tm, d), lambda i, j: (i, 0)), row, row, row,
                  pl.BlockSpec((d, tf), lambda i, j: (0, j)),
                  pl.BlockSpec((d, tf), lambda i, j: (0, j)),
                  pl.BlockSpec((3, tf), lambda i, j: (0, j)),
                  pl.BlockSpec((1, tf), lambda i, j: (0, j)),
                  pl.BlockSpec((tf, d), lambda i, j: (j, 0)),
                  row, row],
        out_specs=pl.BlockSpec((tm, d), lambda i, j: (i, 0)),
        out_shape=jax.ShapeDtypeStruct((s, d), F32),
        scratch_shapes=[pltpu.VMEM((tm, d), BF16), pltpu.VMEM((tm, d), F32), pltpu.VMEM((nj, 8, tf), F32)],
        compiler_params=_params("arbitrary", "arbitrary"),
    )(x, g, sc, sh, wa, wu, cw, cb, wd, gp, gt)


def _proj_weight(w_in):
    d = w_in.shape[0]
    offs = np.cumsum([0, 2 * M_WIDTH, M_WIDTH, M_WIDTH, M_HEADS, M_HEADS, S_WIDTH, N_WIDTH,
                      6 * N_KV_HEADS * N_HEAD_DIM, 3 * N_HEADS])
    seg = [w_in[:, offs[k]:offs[k + 1]] for k in range(9)]
    m_qk, m_v, m_o, m_i, m_f, s_u, n_q, n_kv, n_g = seg
    small = jnp.concatenate([m_i, m_f, n_g], axis=1)
    pad = jnp.zeros((d, N_PROJ_PAD - COL_SMALL - small.shape[1]), w_in.dtype)
    return jnp.concatenate([m_qk, m_v, m_o, s_u, n_q, n_kv, small, pad], axis=1).astype(BF16)


def _pad_cols(a, n):
    return jnp.concatenate([a, jnp.zeros(a.shape[:-1] + (n - a.shape[-1],), a.dtype)], axis=-1)


def kernel(x, c, positions, w_ada, b_ada, g_pre_mix, g_post_mix, g_pre_ffn, g_post_ffn, w_in, w_out, m_conv_w, m_conv_b, m_i_bias, m_f_bias, m_norm_g, s_lam_re, s_lam_im, s_log_dt, s_b_re, s_b_im, s_c_re, s_c_im, s_d, s_w_glu, s_b_glu, n_cmp_pos, n_cmp_w1, n_cmp_w2, f_w_up, f_conv_w, f_conv_b, f_w_down):
    batch, seq, d = x.shape
    assert batch == 1 and d == D_MODEL and seq % 1024 == 0
    depth = w_ada.shape[0]
    xs = x.reshape(seq, d)
    mod = _ada_mod(c.reshape(d, 1), w_ada, b_ada)
    cos_f, sin_f = _rope_tables(positions.reshape(seq, 1))
    row = lambda a: a.reshape(1, -1)
    for l in range(depth):
        sh1, sc1, gt1, sh2, sc2, gt2 = [mod[l, :, k * d:(k + 1) * d] for k in range(6)]
        proj = _inproj(xs, row(g_pre_mix[l]), sc1, sh1, _proj_weight(w_in[l]))
        gate_bias = _pad_cols(jnp.concatenate([m_i_bias[l], m_f_bias[l]]).reshape(1, -1), 128)
        y_m = _mlstm(proj, m_conv_w[l], row(m_conv_b[l]), gate_bias, row(m_norm_g[l]))
        ops = _s5_prep(s_lam_re[l], s_lam_im[l], s_log_dt[l], s_b_re[l], s_b_im[l], s_c_re[l], s_c_im[l])
        y_s = _s5(proj, ops, row(s_d[l]), s_w_glu[l].astype(BF16), row(s_b_glu[l]))
        y_n = _nsa(proj, cos_f, sin_f, n_cmp_pos[l], n_cmp_w1[l], n_cmp_w2[l])
        xs = _outproj(y_m, y_s, y_n, w_out[l].astype(BF16), xs, row(g_post_mix[l]), gt1)
        wa = _pad_cols(f_w_up[l][:, :D_FF], D_FF_PAD).astype(BF16)
        wu = _pad_cols(f_w_up[l][:, D_FF:], D_FF_PAD).astype(BF16)
        wd = jnp.concatenate([f_w_down[l], jnp.zeros((D_FF_PAD - D_FF, d), F32)], axis=0).astype(BF16)
        xs = _ffn(xs, row(g_pre_ffn[l]), sc2, sh2, wa, wu, _pad_cols(f_conv_w[l], D_FF_PAD),
                  _pad_cols(row(f_conv_b[l]), D_FF_PAD), wd, row(g_post_ffn[l]), gt2)
    return xs.reshape(batch, seq, d)
```

```python
import functools

import numpy as np
import jax
import jax.numpy as jnp
from jax import lax
from jax.experimental import pallas as pl
from jax.experimental.pallas import tpu as pltpu

F32 = jnp.float32
BF16 = jnp.bfloat16

D_MODEL = 2048
EPS = 1e-6
M_WIDTH = 512
M_HEADS = 4
M_HEAD_DIM = 128
M_CHUNK = 128
M_CONV = 4
S_WIDTH = 512
S_GROUP = 16
S_GROUPS = 32
S_STATE = 64
S_T = 16
S_OCT = 8
N_WIDTH = 1024
N_HEAD_DIM = 128
N_HEADS = 8
N_KV_HEADS = 2
N_REP = N_HEADS // N_KV_HEADS
CMP_BLOCK = 32
CMP_STRIDE = 16
SLC_BLOCK = 64
SLC_TOPK = 16
WINDOW = 512
Q_BLOCK = 128
ROPE_THETA = 10000.0
NEG_INF = -1e30
FORCED_SCORE = 1e9
SLC_TILE = 512
V_EXTRA = 16
D_FF = 5504
D_FF_PAD = 5632
FFN_TILE = 512
COL_MQK, COL_NQ, COL_MV, COL_MO, COL_SU, COL_NKV, COL_SMALL = 0, 1024, 2048, 2560, 3072, 3584, 5120
N_PROJ_PAD = 5376
PROJ_TILE = 768
SMALL_I, SMALL_F, SMALL_G = 0, 4, 8

VMEM_LIMIT_BYTES = 58 * 1024 * 1024


def _params(*sem):
    return pltpu.CompilerParams(dimension_semantics=sem, vmem_limit_bytes=VMEM_LIMIT_BYTES)


def _gelu(x):
    return 0.5 * x * (1.0 + jnp.tanh(0.7978845608028654 * (x + 0.044715 * (x * x * x))))


def _rms(x):
    return lax.rsqrt(jnp.mean(x * x, axis=-1, keepdims=True) + EPS)


def _dot(a, b):
    return jnp.dot(a, b, preferred_element_type=F32)


def _dot_nt(a, b):
    return lax.dot_general(a, b, (((1,), (1,)), ((), ())), preferred_element_type=F32)


def _ada_kernel(c_ref, w_ref, b_ref, o_ref):
    c = c_ref[...]
    cs = c * jax.nn.sigmoid(c)
    o_ref[...] = jnp.sum(w_ref[...] * cs, axis=0, keepdims=True) + b_ref[...]


def _ada_mod(c_col, w_ada, b_ada):
    depth, d, n = w_ada.shape
    tn = 1024
    return pl.pallas_call(
        _ada_kernel,
        grid=(depth, n // tn),
        in_specs=[pl.BlockSpec((d, 1), lambda l, j: (0, 0)),
                  pl.BlockSpec((None, d, tn), lambda l, j: (l, 0, j)),
                  pl.BlockSpec((None, 1, tn), lambda l, j: (l, 0, j))],
        out_specs=pl.BlockSpec((None, 1, tn), lambda l, j: (l, 0, j)),
        out_shape=jax.ShapeDtypeStruct((depth, 1, n), F32),
        compiler_params=_params("arbitrary", "arbitrary"),
    )(c_col, w_ada, b_ada.reshape(depth, 1, n))


def _inproj_kernel(x_ref, g_ref, sc_ref, sh_ref, w_ref, o_ref, h_ref):
    @pl.when(pl.program_id(1) == 0)
    def _():
        x = x_ref[...]
        h = (x * _rms(x)) * g_ref[...]
        h_ref[...] = (h * (1.0 + sc_ref[...]) + sh_ref[...]).astype(BF16)

    o_ref[...] = _dot(h_ref[...], w_ref[...])


def _inproj(x, g, sc, sh, w):
    s, d = x.shape
    n = w.shape[1]
    tm, tn = min(s, 1024), PROJ_TILE
    row = pl.BlockSpec((1, d), lambda i, j: (0, 0))
    return pl.pallas_call(
        _inproj_kernel,
        grid=(s // tm, n // tn),
        in_specs=[pl.BlockSpec((tm, d), lambda i, j: (i, 0)), row, row, row,
                  pl.BlockSpec((d, tn), lambda i, j: (0, j))],
        out_specs=pl.BlockSpec((tm, tn), lambda i, j: (i, j)),
        out_shape=jax.ShapeDtypeStruct((s, n), F32),
        scratch_shapes=[pltpu.VMEM((tm, d), BF16)],
        compiler_params=_params("arbitrary", "arbitrary"),
    )(x, g, sc, sh, w)


def _mlstm_kernel(qk_ref, v_ref, o_ref, sm_ref, cw_ref, cb_ref, gb_ref, ng_ref, y_ref,
                  tail_ref, c_ref, n_ref, m_ref):
    L, Dh = M_CHUNK, M_HEAD_DIM

    @pl.when(pl.program_id(0) == 0)
    def _():
        tail_ref[...] = jnp.zeros_like(tail_ref)
        c_ref[...] = jnp.zeros_like(c_ref)
        n_ref[...] = jnp.zeros_like(n_ref)
        m_ref[...] = jnp.zeros_like(m_ref)

    raw = qk_ref[...]
    tail = tail_ref[...]
    row = lax.broadcasted_iota(jnp.int32, raw.shape, 0)

    def shifted(d):
        r = pltpu.roll(raw, d, axis=0)
        for k in range(d):
            r = jnp.where(row == k, tail[8 - d + k:8 - d + k + 1, :], r)
        return r

    cw = cw_ref[...]
    pre = (cw[3:4] * raw + cw[2:3] * shifted(1) + cw[1:2] * shifted(2) + cw[0:1] * shifted(3)
           + cb_ref[...])
    tail_ref[...] = raw[L - 8:L, :]
    qk = pre * jax.nn.sigmoid(pre)

    sm = sm_ref[...] + gb_ref[...]
    lf = jnp.minimum(sm, 0.0) - jnp.log1p(jnp.exp(-jnp.abs(sm)))
    rowg = lax.broadcasted_iota(jnp.int32, lf.shape, 0)
    g = lf
    for s in (1, 2, 4, 8, 16, 32, 64):
        g = g + jnp.where(rowg >= s, pltpu.roll(g, s, axis=0), 0.0)
    smT = sm.T
    gT = g.T
    ri = lax.broadcasted_iota(jnp.int32, (L, L), 0)
    ci = lax.broadcasted_iota(jnp.int32, (L, L), 1)
    causal = ri >= ci

    for h in range(M_HEADS):
        hs = slice(h * Dh, (h + 1) * Dh)
        q = qk[:, h * Dh:(h + 1) * Dh]
        k = qk[:, M_WIDTH + h * Dh:M_WIDTH + (h + 1) * Dh] * (Dh ** -0.5)
        v = v_ref[:, hs]
        i_c = sm[:, SMALL_I + h:SMALL_I + h + 1]
        g_c = g[:, SMALL_F + h:SMALL_F + h + 1]
        i_r = smT[SMALL_I + h:SMALL_I + h + 1, :]
        g_r = gT[SMALL_F + h:SMALL_F + h + 1, :]
        g_last = g_r[:, L - 1:L]
        a_r = g_last - g_r + i_r
        a_max = jnp.max(a_r, axis=1, keepdims=True)
        w_c = jnp.exp(g_last - g_c + i_c - a_max)
        m_prev = m_ref[h:h + 1, 0:1]
        c_prev = c_ref[h]
        n_prev = n_ref[h:h + 1, :]
        qb = q.astype(BF16)
        kb = k.astype(BF16)
        vb = v.astype(BF16)
        c_loc = _dot(k.T.astype(BF16), (w_c * v).astype(BF16))
        n_loc = jnp.sum(w_c * k, axis=0, keepdims=True)
        m_new = jnp.maximum(g_last + m_prev, a_max)
        s_old = jnp.exp(g_last + m_prev - m_new)
        s_new = jnp.exp(a_max - m_new)
        c_ref[h] = s_old * c_prev + s_new * c_loc
        n_ref[h:h + 1, :] = s_old * n_prev + s_new * n_loc
        m_ref[h:h + 1, :] = jnp.broadcast_to(m_new, (1, Dh))
        dlog = jnp.where(causal, g_c - g_r + i_r, -jnp.inf)
        m_inter = g_c + m_prev
        m_t = jnp.maximum(m_inter, jnp.max(dlog, axis=1, keepdims=True))
        sw = _dot_nt(qb, kb) * jnp.exp(dlog - m_t)
        inter = jnp.exp(m_inter - m_t)
        num = _dot(sw.astype(BF16), vb) + inter * _dot(qb, c_prev.astype(BF16))
        den = jnp.sum(sw, axis=1, keepdims=True) + inter * jnp.sum(q * n_prev, axis=1, keepdims=True)
        hh = num / jnp.maximum(jnp.abs(den), jnp.exp(-m_t))
        hn = (hh * _rms(hh)) * ng_ref[:, hs]
        y_ref[:, hs] = (jax.nn.sigmoid(o_ref[:, hs]) * hn).astype(BF16)


def _mlstm(proj, conv_w, conv_b, gate_bias, norm_g):
    s = proj.shape[0]
    L = M_CHUNK
    full = lambda shape: pl.BlockSpec(shape, lambda c: (0,) * len(shape))
    return pl.pallas_call(
        _mlstm_kernel,
        grid=(s // L,),
        in_specs=[pl.BlockSpec((L, 2 * M_WIDTH), lambda c: (c, COL_MQK // (2 * M_WIDTH))),
                  pl.BlockSpec((L, M_WIDTH), lambda c: (c, COL_MV // M_WIDTH)),
                  pl.BlockSpec((L, M_WIDTH), lambda c: (c, COL_MO // M_WIDTH)),
                  pl.BlockSpec((L, 128), lambda c: (c, COL_SMALL // 128)),
                  full((M_CONV, 2 * M_WIDTH)), full((1, 2 * M_WIDTH)), full((1, 128)), full((1, M_WIDTH))],
        out_specs=pl.BlockSpec((L, M_WIDTH), lambda c: (c, 0)),
        out_shape=jax.ShapeDtypeStruct((s, M_WIDTH), BF16),
        scratch_shapes=[pltpu.VMEM((8, 2 * M_WIDTH), F32),
                        pltpu.VMEM((M_HEADS, M_HEAD_DIM, M_HEAD_DIM), F32),
                        pltpu.VMEM((8, M_HEAD_DIM), F32),
                        pltpu.VMEM((8, M_HEAD_DIM), F32)],
        compiler_params=_params("arbitrary"),
    )(proj, proj, proj, proj, conv_w, conv_b, gate_bias, norm_g)


def _split3(a):
    hi = a.astype(BF16)
    r = a - hi.astype(F32)
    mid = r.astype(BF16)
    lo = (r - mid.astype(F32)).astype(BF16)
    return hi, mid, lo


def _dot_f32(a, b):
    ah, am, al = _split3(a)
    bh, bm, bl = _split3(b)
    small = _dot(ah, bl) + _dot(am, bm) + _dot(al, bh)
    mid = _dot(ah, bm) + _dot(am, bh)
    return _dot(ah, bh) + (mid + small)


def _s5_prep_kernel(lrr_ref, lir_ref, lrc_ref, lic_ref, ldt_ref, btr_ref, bti_ref, ctr_ref, cti_ref, place_ref,
                    m_ref, pr_ref, pi_ref, rr_ref, ri_ref, ar_ref, ai_ref):
    T = S_T
    dt = jnp.exp(ldt_ref[...])
    lr, li = lrr_ref[...], lir_ref[...]
    mag = jnp.exp(lr * dt)
    a_re = mag * jnp.cos(li * dt)
    a_im = mag * jnp.sin(li * dt)
    den = lr * lr + li * li
    nr = a_re - 1.0
    z_re = (nr * lr + a_im * li) / den
    z_im = (a_im * lr - nr * li) / den
    btr, bti = btr_ref[...], bti_ref[...]
    bb_re = z_re * btr - z_im * bti
    bb_im = z_re * bti + z_im * btr

    def pow_row(e):
        mg = jnp.exp((lr * dt) * e)
        th = (li * dt) * e
        return mg * jnp.cos(th), mg * jnp.sin(th)

    jrow = lax.shift_right_logical(lax.broadcasted_iota(jnp.int32, btr.shape, 0), 4).astype(F32)
    p_re, p_im = pow_row(-jrow)
    left_re = bb_re * p_re - bb_im * p_im
    left_im = bb_re * p_im + bb_im * p_re
    q_re, q_im = pow_row((T - 1.0) - jrow)
    pr_ref[...] = bb_re * q_re - bb_im * q_im
    pi_ref[...] = bb_re * q_im + bb_im * q_re

    lrc, lic = lrc_ref[...], lic_ref[...]
    ctr, cti = ctr_ref[...], cti_ref[...]

    def pow_col(e):
        mg = jnp.exp((lrc * dt) * e)
        th = (lic * dt) * e
        return mg * jnp.cos(th), mg * jnp.sin(th)

    icol = lax.shift_right_logical(lax.broadcasted_iota(jnp.int32, ctr.shape, 1), 4).astype(F32)
    e_re, e_im = pow_col(icol)
    right_re = ctr * e_re - cti * e_im
    right_im = ctr * e_im + cti * e_re
    m = _dot_f32(left_re, right_re) - _dot_f32(left_im, right_im)
    mj = lax.shift_right_logical(lax.broadcasted_iota(jnp.int32, m.shape, 0), 4)
    mi = lax.shift_right_logical(lax.broadcasted_iota(jnp.int32, m.shape, 1), 4)
    m_ref[...] = _dot(jnp.where(mi >= mj, m, 0.0).astype(BF16), place_ref[...]).astype(BF16)
    f_re, f_im = pow_col(icol + 1.0)
    rr_ref[...] = ctr * f_re - cti * f_im
    ri_ref[...] = -(ctr * f_im + cti * f_re)
    mag_t = jnp.exp((lr * dt) * T)
    ar_ref[...] = mag_t * jnp.cos((li * dt) * T)
    ai_ref[...] = mag_t * jnp.sin((li * dt) * T)


def _s5_prep(lam_re, lam_im, log_dt, b_re, b_im, c_re, c_im):
    G, N, C, T = S_GROUPS, S_STATE, S_GROUP, S_T
    TC = T * C
    bt = lambda b: jnp.tile(jnp.swapaxes(b, 1, 2), (1, T, 1))
    ct = lambda c: jnp.tile(jnp.swapaxes(c, 1, 2), (1, 1, T))
    blk = lambda a, b: pl.BlockSpec((None, a, b), lambda g: (g, 0, 0))
    O, NO = S_OCT, G // S_OCT
    W = T * O * C
    col = np.arange(W)
    src = (col // (O * C)) * C + col % C
    place = np.stack([(np.arange(TC)[:, None] == src[None, :]) & ((col // C) % O == g)[None, :]
                      for g in range(O)]).astype(np.float32)
    outs = pl.pallas_call(
        _s5_prep_kernel,
        grid=(G,),
        in_specs=[blk(1, N), blk(1, N), blk(N, 1), blk(N, 1), blk(1, 1),
                  blk(TC, N), blk(TC, N), blk(N, TC), blk(N, TC),
                  pl.BlockSpec((None, TC, W), lambda g: (g % O, 0, 0))],
        out_specs=[blk(TC, W), blk(TC, N), blk(TC, N), blk(N, TC), blk(N, TC), blk(1, N), blk(1, N)],
        out_shape=[jax.ShapeDtypeStruct((G, TC, W), BF16),
                   jax.ShapeDtypeStruct((G, TC, N), F32), jax.ShapeDtypeStruct((G, TC, N), F32),
                   jax.ShapeDtypeStruct((G, N, TC), F32), jax.ShapeDtypeStruct((G, N, TC), F32),
                   jax.ShapeDtypeStruct((G, 1, N), F32), jax.ShapeDtypeStruct((G, 1, N), F32)],
        compiler_params=_params("arbitrary"),
    )(lam_re.reshape(G, 1, N), lam_im.reshape(G, 1, N), lam_re.reshape(G, N, 1), lam_im.reshape(G, N, 1),
      log_dt.reshape(G, 1, 1), bt(b_re), bt(b_im), ct(c_re), ct(c_im), jnp.asarray(place, BF16))
    m, pr, pi, rr, ri, ar, ai = outs

    m_oct = m.reshape(NO, O, T, C, W).transpose(0, 2, 1, 3, 4).reshape(NO, W, W)

    def p_oct(p):
        p5 = p.astype(BF16).reshape(NO, O, T, C, N)
        out = jnp.zeros((NO, T, O, C, O, N), BF16)
        for g in range(O):
            out = out.at[:, :, g, :, g, :].set(p5[:, g])
        return out.reshape(NO, W, O * N)

    def r_oct(r):
        r5 = r.astype(BF16).reshape(NO, O, N, T, C)
        out = jnp.zeros((NO, O, N, T, O, C), BF16)
        for g in range(O):
            out = out.at[:, g, :, :, g, :].set(r5[:, g])
        return out.reshape(NO, O * N, W)

    oct_row = lambda a: a.reshape(NO, 1, O * N)
    return (m_oct, p_oct(pr), p_oct(pi), r_oct(rr), r_oct(ri), oct_row(ar), oct_row(ai))


def _s5_kernel(u_ref, m_ref, pr_ref, pi_ref, rr_ref, ri_ref, ar_ref, ai_ref, y_ref,
               xr_s, xi_s, er_s, ei_s, st_s):
    T = S_T
    nch = u_ref.shape[0] // T

    @pl.when(pl.program_id(1) == 0)
    def _():
        st_s[...] = jnp.zeros_like(st_s)

    u = jnp.concatenate([u_ref[pl.ds(j, nch, stride=T), :] for j in range(T)], axis=1).astype(BF16)
    y_loc = _dot(u, m_ref[...])
    xr_s[...] = _dot(u, pr_ref[...])
    xi_s[...] = _dot(u, pi_ref[...])
    a_re, a_im = ar_ref[...], ai_ref[...]

    def body(b, carry):
        e_re, e_im = carry
        base = pl.multiple_of(b * 8, 8)
        xr = xr_s[pl.ds(base, 8), :]
        xi = xi_s[pl.ds(base, 8), :]
        for k in range(8):
            er_s[pl.ds(base + k, 1), :] = e_re
            ei_s[pl.ds(base + k, 1), :] = e_im
            n_re = a_re * e_re - a_im * e_im + xr[k:k + 1]
            n_im = a_re * e_im + a_im * e_re + xi[k:k + 1]
            e_re, e_im = n_re, n_im
        return e_re, e_im

    e_re, e_im = lax.fori_loop(0, nch // 8, body, (st_s[0:1, :], st_s[1:2, :]))
    st_s[0:1, :] = e_re
    st_s[1:2, :] = e_im
    y = (y_loc + _dot(er_s[...].astype(BF16), rr_ref[...]) + _dot(ei_s[...].astype(BF16), ri_ref[...]))
    lanes = y_ref.shape[1]
    for i in range(T):
        y_ref[pl.ds(i, nch, stride=T), :] = y[:, i * lanes:(i + 1) * lanes]


def _s5_scan(proj, ops):
    m, pr, pi, rr, ri, ar, ai = ops
    s = proj.shape[0]
    no, w, ns = pr.shape
    lanes = S_OCT * S_GROUP
    ts = min(s, 4096)
    nch = ts // S_T
    op = lambda a, b: pl.BlockSpec((None, a, b), lambda o, r: (o, 0, 0))
    return pl.pallas_call(
        _s5_kernel,
        grid=(no, s // ts),
        in_specs=[pl.BlockSpec((ts, lanes), lambda o, r: (r, COL_SU // lanes + o)),
                  op(w, w), op(w, ns), op(w, ns), op(ns, w), op(ns, w), op(1, ns), op(1, ns)],
        out_specs=pl.BlockSpec((None, ts, lanes), lambda o, r: (o, r, 0)),
        out_shape=jax.ShapeDtypeStruct((no, s, lanes), F32),
        scratch_shapes=[pltpu.VMEM((nch, ns), F32) for _ in range(4)] + [pltpu.VMEM((8, ns), F32)],
        compiler_params=_params("arbitrary", "arbitrary"),
    )(proj, m, pr, pi, rr, ri, ar, ai)


def _s5_glu_kernel(y_ref, u_ref, d_ref, w_ref, b_ref, o_ref):
    y = jnp.concatenate([y_ref[o] for o in range(y_ref.shape[0])], axis=1)
    y = _gelu(y + d_ref[...] * u_ref[...])
    z = _dot(y.astype(BF16), w_ref[...]) + b_ref[...]
    o_ref[...] = (y * jax.nn.sigmoid(z)).astype(BF16)


def _s5_glu(y, proj, d_skip, w_glu, b_glu):
    no, s, lanes = y.shape
    w = no * lanes
    tm = 1024
    row = pl.BlockSpec((1, w), lambda i: (0, 0))
    return pl.pallas_call(
        _s5_glu_kernel,
        grid=(s // tm,),
        in_specs=[pl.BlockSpec((no, tm, lanes), lambda i: (0, i, 0)),
                  pl.BlockSpec((tm, w), lambda i: (i, COL_SU // S_WIDTH)),
                  row, pl.BlockSpec((w, w), lambda i: (0, 0)), row],
        out_specs=pl.BlockSpec((tm, w), lambda i: (i, 0)),
        out_shape=jax.ShapeDtypeStruct((s, w), BF16),
        compiler_params=_params("arbitrary"),
    )(y, proj, d_skip, w_glu, b_glu)


def _s5(proj, ops, d_skip, w_glu, b_glu):
    return _s5_glu(_s5_scan(proj, ops), proj, d_skip, w_glu, b_glu)


def _rope_tab_kernel(pos_ref, inv_ref, sgn_ref, cos_ref, sin_ref):
    ang = pos_ref[...].astype(F32) * inv_ref[...]
    cos_ref[...] = jnp.cos(ang)
    sin_ref[...] = jnp.sin(ang) * sgn_ref[...]


def _rope_tables(pos_col):
    s = pos_col.shape[0]
    half = N_HEAD_DIM // 2
    inv = (ROPE_THETA ** (-jnp.arange(half, dtype=F32) / half))
    inv = jnp.concatenate([inv, inv]).reshape(1, N_HEAD_DIM)
    sgn = jnp.concatenate([-jnp.ones((half,), F32), jnp.ones((half,), F32)]).reshape(1, N_HEAD_DIM)
    tm = min(s, 2048)
    row = pl.BlockSpec((1, N_HEAD_DIM), lambda i: (0, 0))
    tab = pl.BlockSpec((tm, N_HEAD_DIM), lambda i: (i, 0))
    return pl.pallas_call(
        _rope_tab_kernel,
        grid=(s // tm,),
        in_specs=[pl.BlockSpec((tm, 1), lambda i: (i, 0)), row, row],
        out_specs=[tab, tab],
        out_shape=[jax.ShapeDtypeStruct((s, N_HEAD_DIM), F32)] * 2,
        compiler_params=_params("arbitrary"),
    )(pos_col, inv, sgn)


def _col_reduce(x, op):
    while x.shape[0] > 8 and x.shape[0] % 64 == 0:
        x = op(x.reshape(8, x.shape[0] // 8, x.shape[1]), axis=0)
    return op(x, axis=0, keepdims=True)


def _colmax(x):
    return _col_reduce(x, jnp.max)


def _colsum(x):
    return _col_reduce(x, jnp.sum)


def _rope(x, cos_f, sin_f):
    return x * cos_f + pltpu.roll(x, N_HEAD_DIM // 2, axis=1) * sin_f


def _nsa_pre_kernel(*refs):
    kv = refs[0:6 * N_KV_HEADS]
    cos_ref, sin_ref = refs[12], refs[13]
    kcmp_o, vcmp_o, kslc_o, vslct_o, kwin_o, vwint_o = refs[14:]
    cos_f, sin_f = cos_ref[...], sin_ref[...]
    tm = cos_f.shape[0]
    key = pl.program_id(0) * tm + lax.broadcasted_iota(jnp.int32, (tm, N_HEAD_DIM), 0)
    blk = lax.shift_right_logical(key, 6) & (SLC_TILE // SLC_BLOCK - 1)
    onehot = jnp.where(lax.broadcasted_iota(jnp.int32, (tm, N_HEAD_DIM), 1) == blk, 1.0, 0.0).astype(BF16)
    ones_rows = jnp.where(lax.broadcasted_iota(jnp.int32, (V_EXTRA, tm), 0) == 0, 1.0, 0.0)
    pieces = [pl.ds(r, tm // CMP_STRIDE, stride=CMP_STRIDE) for r in range(CMP_STRIDE)]
    for kh in range(N_KV_HEADS):
        kcmp_o[kh] = jnp.concatenate(
            [_rope(kv[0 * N_KV_HEADS + kh][rows, :], cos_ref[rows, :], sin_ref[rows, :]) for rows in pieces], axis=1)
        vcmp_o[kh] = jnp.concatenate([kv[1 * N_KV_HEADS + kh][rows, :] for rows in pieces], axis=1)
        kslc_o[kh] = jnp.concatenate(
            [_rope(kv[2 * N_KV_HEADS + kh][...], cos_f, sin_f).astype(BF16), onehot], axis=1)
        vslct_o[kh] = jnp.concatenate([kv[3 * N_KV_HEADS + kh][...].T, ones_rows], axis=0).astype(BF16)
        kwin_o[kh] = _rope(kv[4 * N_KV_HEADS + kh][...], cos_f, sin_f).astype(BF16)
        vwint_o[kh] = jnp.concatenate([kv[5 * N_KV_HEADS + kh][...].T, ones_rows], axis=0).astype(BF16)


def _nsa_pre(proj, cos_f, sin_f):
    s = proj.shape[0]
    Hk, Dh = N_KV_HEADS, N_HEAD_DIM
    tm = 512
    kv_specs = [pl.BlockSpec((tm, Dh), functools.partial(lambda i, c: (i, c), c=COL_NKV // Dh + n))
                for n in range(6 * Hk)]
    tab = pl.BlockSpec((tm, Dh), lambda i: (i, 0))
    std = pl.BlockSpec((Hk, tm, Dh), lambda i: (0, i, 0))
    tr = pl.BlockSpec((Hk, Dh + V_EXTRA, tm), lambda i: (0, 0, i))
    ext = pl.BlockSpec((Hk, tm, 2 * Dh), lambda i: (0, i, 0))
    pcs = pl.BlockSpec((Hk, tm // CMP_STRIDE, CMP_STRIDE * Dh), lambda i: (0, i, 0))
    return pl.pallas_call(
        _nsa_pre_kernel,
        grid=(s // tm,),
        in_specs=kv_specs + [tab, tab],
        out_specs=[pcs, pcs, ext, tr, std, tr],
        out_shape=[jax.ShapeDtypeStruct((Hk, s // CMP_STRIDE, CMP_STRIDE * Dh), F32),
                   jax.ShapeDtypeStruct((Hk, s // CMP_STRIDE, CMP_STRIDE * Dh), F32),
                   jax.ShapeDtypeStruct((Hk, s, 2 * Dh), BF16), jax.ShapeDtypeStruct((Hk, Dh + V_EXTRA, s), BF16),
                   jax.ShapeDtypeStruct((Hk, s, Dh), BF16), jax.ShapeDtypeStruct((Hk, Dh + V_EXTRA, s), BF16)],
        compiler_params=_params("arbitrary"),
    )(*([proj] * (6 * Hk)), cos_f, sin_f)


def _cmp_kernel(x_ref, p1_ref, p2_ref, wt_ref, wb_ref, w2_ref, o_ref):
    x = x_ref[...]
    first = _dot((x + p1_ref[...]).astype(BF16), wt_ref[...])
    second = _dot((x + p2_ref[...]).astype(BF16), wb_ref[...])
    npieces = x.shape[0]
    pre = first + pltpu.roll(second, npieces - 1, axis=0)
    o_ref[...] = _dot(_gelu(pre).astype(BF16), w2_ref[...])


def _compress(xp, pos, w1, w2):
    hk, npieces, half = xp.shape
    dh = half // CMP_STRIDE
    p1 = pos[:CMP_STRIDE].reshape(1, half)
    p2 = pos[CMP_STRIDE:].reshape(1, half)
    full = lambda shape: pl.BlockSpec(shape, lambda h: (0,) * len(shape))
    return pl.pallas_call(
        _cmp_kernel,
        grid=(hk,),
        in_specs=[pl.BlockSpec((None, npieces, half), lambda h: (h, 0, 0)),
                  full((1, half)), full((1, half)), full((half, dh)), full((half, dh)), full((dh, dh))],
        out_specs=pl.BlockSpec((None, npieces, dh), lambda h: (h, 0, 0)),
        out_shape=jax.ShapeDtypeStruct((hk, npieces, dh), F32),
        compiler_params=_params("arbitrary"),
    )(xp, p1, p2, w1[:half].astype(BF16), w1[half:].astype(BF16), w2.astype(BF16))


def _nsa_attn_kernel(*refs, seq):
    heads = [_nsa_head(kh, refs, seq) for kh in range(N_KV_HEADS)]
    n_tiles = heads[0][0]

    def slc_pair(k2, carry):
        for parity in (0, 1):
            stages = [head[1 + parity](2 * k2 + parity) for head in heads]
            for stage in range(3):
                for head_stages in stages:
                    head_stages[stage]()
        return carry

    lax.fori_loop(0, (n_tiles + 1) // 2, slc_pair, 0)
    for head in heads:
        head[3]()


def _nsa_head(kh, refs, seq):
    NW = WINDOW // Q_BLOCK + 1
    q_ref, sm_ref, cos_ref, sin_ref, cend_ref = refs[:5]
    kc_ref, vct_ref, ks_ref, vst_ref, ksd_ref, vsd_ref = [r.at[kh] for r in refs[5:11]]
    kw_refs = [r.at[kh] for r in refs[11:11 + NW]]
    vw_refs = [r.at[kh] for r in refs[11 + NW:11 + 2 * NW]]
    y_ref, gate_s = refs[11 + 2 * NW:13 + 2 * NW]
    (bias_s, rhs_s, sa_s, sb_s, pa_s, pb_s, aa_s, ab_s, m_s, acc_s, ocmp_s, owin_s
     ) = [r.at[kh] for r in refs[13 + 2 * NW:]]
    Dh, QB, G, TK = N_HEAD_DIM, Q_BLOCK, N_REP, SLC_TILE
    n_slc = seq // SLC_BLOCK
    n_cmp = seq // CMP_STRIDE
    ratio = SLC_BLOCK // CMP_STRIDE
    qb = pl.program_id(0)
    t0 = qb * QB
    c2 = (Dh ** -0.5) * 1.4426950408889634
    cos_f, sin_f = cos_ref[...], sin_ref[...]
    q = q_ref[:, kh * G * Dh:(kh + 1) * G * Dh]
    q_t = jnp.concatenate(
        [(_rope(q[:, g * Dh:(g + 1) * Dh], cos_f, sin_f) * c2).T.astype(BF16) for g in range(G)],
        axis=1)
    if kh == 0:
        gate_s[...] = jax.nn.sigmoid(sm_ref[...]).T
    t_lane = t0 + lax.broadcasted_iota(jnp.int32, (1, QB), 1)

    s_all = _dot(kc_ref[...], q_t)
    valid = cend_ref[...] <= t0
    any_valid = t_lane >= CMP_BLOCK - 1
    pk = jnp.zeros((n_cmp, QB), F32)
    p_parts = []
    for g in range(G):
        s_g = jnp.where(valid, s_all[:, g * QB:(g + 1) * QB], NEG_INF)
        e = jnp.exp2(s_g - _colmax(s_g))
        p = e * jnp.where(any_valid, 1.0 / _colsum(e), 0.0)
        pk = pk + p
        p_parts.append(p.astype(BF16))
    o_cmp = _dot(vct_ref[...], jnp.concatenate(p_parts, axis=1))

    s = _dot(jnp.concatenate([r[...] for r in kw_refs], axis=0), q_t)
    wpos = t0 - WINDOW + lax.broadcasted_iota(jnp.int32, (WINDOW + QB, QB), 0)
    wbias = jnp.where(wpos <= t_lane, jnp.where(wpos > t_lane - WINDOW, jnp.where(wpos >= 0, 0.0, NEG_INF),
                                                NEG_INF), NEG_INF)
    s = s + jnp.concatenate([wbias] * G, axis=1)
    p = jnp.exp2(s - _colmax(s))
    o_win = _dot(jnp.concatenate([r[...] for r in vw_refs], axis=1), p.astype(BF16))
    o_win = o_win[0:Dh] / o_win[Dh:Dh + 1]

    rhs_s[0:Dh, :] = q_t
    rhs_s[Dh:2 * Dh, :] = jnp.zeros((Dh, G * QB), BF16)

    slabs = [pk[r * n_slc:(r + 1) * n_slc] for r in range(ratio)]
    row_a = lax.broadcasted_iota(jnp.int32, (n_slc, QB), 0)
    prev_last = jnp.where(row_a == 0, 0.0, pltpu.roll(slabs[ratio - 1], 1, axis=0))
    mid = slabs[0]
    for r in range(1, ratio - 1):
        mid = mid + slabs[r]
    imp = (prev_last + 2.0 * mid) + slabs[ratio - 1]
    tb = lax.shift_right_logical(t_lane, 6)
    allowed = row_a <= tb
    forced = (row_a == 0) | (row_a == tb) | (row_a == tb - 1)
    score = jnp.where(forced, -jnp.inf, jnp.where(allowed, imp, -jnp.inf))
    row_f = row_a.astype(F32)
    for _ in range(max(min(SLC_TOPK, n_slc) - 3, 0)):
        best = _colmax(score)
        idx = _col_reduce(jnp.where(score == best, row_f, float(n_slc)), jnp.min)
        score = jnp.where(row_f == idx, -jnp.inf, score)
    bias_s[...] = jnp.where(score == -jnp.inf, jnp.where(row_a < 2 * qb, 0.0, NEG_INF), NEG_INF)

    blocks_per_tile = TK // SLC_BLOCK
    n_tiles = (t0 + TK - 1) // TK

    def scores(kt, buf):
        live = kt < n_tiles
        kt = jnp.maximum(jnp.minimum(kt, n_tiles - 1), 0)
        b8 = bias_s[pl.ds(pl.multiple_of(kt * blocks_per_tile, blocks_per_tile), blocks_per_tile), :]
        b8 = jnp.where(live, b8, NEG_INF)
        rows = jnp.concatenate([jnp.concatenate([b8] * G, axis=1),
                                jnp.zeros((16 - blocks_per_tile, G * QB), F32)], axis=0)
        rhs_s[Dh:Dh + 16, :] = rows.astype(BF16)
        buf[...] = _dot(ks_ref[pl.ds(pl.multiple_of(kt * TK, TK), TK), :], rhs_s[...])

    def softmax(buf, p_buf, a_buf):
        s = buf[...]
        m_old = m_s[...]
        m_new = jnp.maximum(m_old, _colmax(s))
        a_buf[...] = jnp.exp2(m_old - m_new)
        p_buf[...] = jnp.exp2(s - m_new).astype(BF16)
        m_s[...] = m_new

    def pv_acc(kt, p_buf, a_buf):
        kt = jnp.maximum(jnp.minimum(kt, n_tiles - 1), 0)
        acc_s[...] = a_buf[...] * acc_s[...] + _dot(vst_ref[:, pl.ds(pl.multiple_of(kt * TK, TK), TK)], p_buf[...])

    s = _dot(ksd_ref[:, 0:Dh], q_t)
    kpos = t0 + lax.broadcasted_iota(jnp.int32, (QB, QB), 0)
    s = s + jnp.concatenate([jnp.where(kpos <= t_lane, 0.0, NEG_INF)] * G, axis=1)
    m0 = _colmax(s)
    p = jnp.exp2(s - m0)
    m_s[...] = m0
    acc_s[...] = _dot(vsd_ref[...], p.astype(BF16))

    scores(0, sa_s)
    scores(1, sb_s)
    softmax(sa_s, pa_s, aa_s)

    ocmp_s[...] = o_cmp
    owin_s[...] = o_win

    def finish():
        o_slc = acc_s[0:Dh, :] / acc_s[Dh:Dh + 1, :]
        outs = []
        for g in range(G):
            base = SMALL_G + (kh * G + g) * 3
            sl = slice(g * QB, (g + 1) * QB)
            o = (gate_s[base:base + 1, :] * ocmp_s[:, sl] + gate_s[base + 1:base + 2, :] * o_slc[:, sl]
                 + gate_s[base + 2:base + 3, :] * owin_s[:, sl])
            outs.append(o.T)
        y_ref[:, kh * G * Dh:(kh + 1) * G * Dh] = jnp.concatenate(outs, axis=1).astype(BF16)

    def half(kt, s_cur, p_cur, a_cur, s_nxt, p_nxt, a_nxt):
        return (lambda: scores(kt + 2, s_cur), lambda: softmax(s_nxt, p_nxt, a_nxt),
                lambda: pv_acc(kt, p_cur, a_cur))

    even = (sa_s, pa_s, aa_s)
    odd = (sb_s, pb_s, ab_s)
    return n_tiles, lambda kt: half(kt, *even, *odd), lambda kt: half(kt, *odd, *even), finish


def _nsa_attn(proj, cos_f, sin_f, kc, vct, ks, vst, kw, vwt):
    s = proj.shape[0]
    Hk, Dh, QB, G = N_KV_HEADS, N_HEAD_DIM, Q_BLOCK, N_REP
    n_cmp = s // CMP_STRIDE
    nw = WINDOW // QB + 1
    ratio = SLC_BLOCK // CMP_STRIDE
    row = np.arange(n_cmp)
    j_cmp = (row % (n_cmp // ratio)) * ratio + row // (n_cmp // ratio)
    cmp_end = jnp.asarray((j_cmp * CMP_STRIDE + (CMP_BLOCK - 1))[:, None] - np.arange(QB)[None, :], jnp.int32)
    whole = lambda a, b: pl.BlockSpec((Hk, a, b), lambda qb: (0, 0, 0), pipeline_mode=pl.Buffered(1))
    tab = pl.BlockSpec((QB, Dh), lambda qb: (qb, 0))
    kw_specs = [pl.BlockSpec((Hk, QB, Dh), functools.partial(
        lambda qb, w: (0, jnp.maximum(qb - (nw - 1) + w, 0), 0), w=w)) for w in range(nw)]
    vw_specs = [pl.BlockSpec((Hk, Dh + V_EXTRA, QB), functools.partial(
        lambda qb, w: (0, 0, jnp.maximum(qb - (nw - 1) + w, 0)), w=w)) for w in range(nw)]
    lanes = G * QB
    return pl.pallas_call(
        functools.partial(_nsa_attn_kernel, seq=s),
        grid=(s // QB,),
        in_specs=[pl.BlockSpec((QB, N_WIDTH), lambda qb: (qb, COL_NQ // N_WIDTH)),
                  pl.BlockSpec((QB, 128), lambda qb: (qb, COL_SMALL // 128)),
                  tab, tab, pl.BlockSpec((n_cmp, QB), lambda qb: (0, 0), pipeline_mode=pl.Buffered(1)),
                  whole(n_cmp, Dh), whole(Dh, n_cmp), whole(s, 2 * Dh), whole(Dh + V_EXTRA, s),
                  pl.BlockSpec((Hk, QB, 2 * Dh), lambda qb: (0, qb, 0)),
                  pl.BlockSpec((Hk, Dh + V_EXTRA, QB), lambda qb: (0, 0, qb))] + kw_specs + vw_specs,
        out_specs=pl.BlockSpec((QB, N_WIDTH), lambda qb: (qb, 0)),
        out_shape=jax.ShapeDtypeStruct((s, N_WIDTH), BF16),
        scratch_shapes=[pltpu.VMEM((128, QB), F32),
                        pltpu.VMEM((Hk, s // SLC_BLOCK, QB), F32),
                        pltpu.VMEM((Hk, 2 * Dh, lanes), BF16),
                        pltpu.VMEM((Hk, SLC_TILE, lanes), F32),
                        pltpu.VMEM((Hk, SLC_TILE, lanes), F32),
                        pltpu.VMEM((Hk, SLC_TILE, lanes), BF16),
                        pltpu.VMEM((Hk, SLC_TILE, lanes), BF16),
                        pltpu.VMEM((Hk, 1, lanes), F32),
                        pltpu.VMEM((Hk, 1, lanes), F32),
                        pltpu.VMEM((Hk, 1, lanes), F32),
                        pltpu.VMEM((Hk, Dh + V_EXTRA, lanes), F32),
                        pltpu.VMEM((Hk, Dh, lanes), F32),
                        pltpu.VMEM((Hk, Dh, lanes), F32)],
        compiler_params=_params("arbitrary"),
    )(proj, proj, cos_f, sin_f, cmp_end, kc, vct, ks, vst, ks, vst, *([kw] * nw), *([vwt] * nw))


def _slab_major(a):
    hk, n, dh = a.shape
    ratio = SLC_BLOCK // CMP_STRIDE
    return a.reshape(hk, n // ratio, ratio, dh).transpose(0, 2, 1, 3).reshape(hk, n, dh)


def _nsa(proj, cos_f, sin_f, cmp_pos, cmp_w1, cmp_w2):
    kcmp, vcmp, ks, vst, kw, vwt = _nsa_pre(proj, cos_f, sin_f)
    kc = _slab_major(_compress(kcmp, cmp_pos[0], cmp_w1[0], cmp_w2[0])).astype(BF16)
    vc = _slab_major(_compress(vcmp, cmp_pos[1], cmp_w1[1], cmp_w2[1])).astype(BF16)
    return _nsa_attn(proj, cos_f, sin_f, kc, jnp.swapaxes(vc, 1, 2), ks, vst, kw, vwt)


def _outproj_kernel(ym_ref, ys_ref, yn_ref, w_ref, x_ref, g_ref, gt_ref, o_ref):
    y = (_dot(ym_ref[...], w_ref[0:M_WIDTH, :]) + _dot(ys_ref[...], w_ref[M_WIDTH:M_WIDTH + S_WIDTH, :])
         + _dot(yn_ref[...], w_ref[M_WIDTH + S_WIDTH:, :]))
    o_ref[...] = x_ref[...] + gt_ref[...] * ((y * _rms(y)) * g_ref[...])


def _outproj(ym, ys, yn, w, x, g, gt):
    s, d = x.shape
    tm = 512
    row = pl.BlockSpec((1, d), lambda i: (0, 0))
    rows = lambda width: pl.BlockSpec((tm, width), lambda i: (i, 0))
    return pl.pallas_call(
        _outproj_kernel,
        grid=(s // tm,),
        in_specs=[rows(M_WIDTH), rows(S_WIDTH), rows(N_WIDTH), pl.BlockSpec((d, d), lambda i: (0, 0)),
                  rows(d), row, row],
        out_specs=rows(d),
        out_shape=jax.ShapeDtypeStruct((s, d), F32),
        compiler_params=_params("arbitrary"),
    )(ym, ys, yn, w, x, g, gt)


def _ffn_kernel(x_ref, g_ref, sc_ref, sh_ref, wa_ref, wu_ref, cw_ref, cb_ref, wd_ref, gp_ref, gt_ref,
                o_ref, h_ref, acc_ref, carry_ref):
    i = pl.program_id(0)
    j = pl.program_id(1)
    tm = x_ref.shape[0]

    @pl.when(j == 0)
    def _():
        x = x_ref[...]
        h = (x * _rms(x)) * g_ref[...]
        h_ref[...] = (h * (1.0 + sc_ref[...]) + sh_ref[...]).astype(BF16)
        acc_ref[...] = jnp.zeros_like(acc_ref)

    @pl.when(i == 0)
    def _():
        carry_ref[j] = jnp.zeros(carry_ref.shape[1:], F32)

    h = h_ref[...]
    a = _dot(h, wa_ref[...])
    u = _dot(h, wu_ref[...])
    prev = carry_ref[j]
    carry_ref[j] = a[tm - 8:tm, :]
    row = lax.broadcasted_iota(jnp.int32, a.shape, 0)
    a1 = jnp.where(row == 0, prev[7:8, :], pltpu.roll(a, 1, axis=0))
    a2 = jnp.where(row == 0, prev[6:7, :], jnp.where(row == 1, prev[7:8, :], pltpu.roll(a, 2, axis=0)))
    cw = cw_ref[...]
    conv = cw[0:1] * a2 + cw[1:2] * a1 + cw[2:3] * a + cb_ref[...]
    acc_ref[...] += _dot((_gelu(conv) * u).astype(BF16), wd_ref[...])

    @pl.when(j == pl.num_programs(1) - 1)
    def _():
        y = acc_ref[...]
        o_ref[...] = x_ref[...] + gt_ref[...] * ((y * _rms(y)) * gp_ref[...])


def _ffn(x, g, sc, sh, wa, wu, cw, cb, wd, gp, gt):
    s, d = x.shape
    fp = wa.shape[1]
    tm, tf = 512, FFN_TILE
    nj = fp // tf
    row = pl.BlockSpec((1, d), lambda i, j: (0, 0))
    return pl.pallas_call(
        _ffn_kernel,
        grid=(s // tm, nj),
        in_specs=[pl.BlockSpec((tm, d), lambda i, j: (i, 0)), row, row, row,
                  pl.BlockSpec((d, tf), lambda i, j: (0, j)),
                  pl.BlockSpec((d, tf), lambda i, j: (0, j)),
                  pl.BlockSpec((3, tf), lambda i, j: (0, j)),
                  pl.BlockSpec((1, tf), lambda i, j: (0, j)),
                  pl.BlockSpec((tf, d), lambda i, j: (j, 0)),
                  row, row],
        out_specs=pl.BlockSpec((tm, d), lambda i, j: (i, 0)),
        out_shape=jax.ShapeDtypeStruct((s, d), F32),
        scratch_shapes=[pltpu.VMEM((tm, d), BF16), pltpu.VMEM((tm, d), F32), pltpu.VMEM((nj, 8, tf), F32)],
        compiler_params=_params("arbitrary", "arbitrary"),
    )(x, g, sc, sh, wa, wu, cw, cb, wd, gp, gt)


def _proj_weight(w_in):
    d = w_in.shape[0]
    offs = np.cumsum([0, 2 * M_WIDTH, M_WIDTH, M_WIDTH, M_HEADS, M_HEADS, S_WIDTH, N_WIDTH,
                      6 * N_KV_HEADS * N_HEAD_DIM, 3 * N_HEADS])
    seg = [w_in[:, offs[k]:offs[k + 1]] for k in range(9)]
    m_qk, m_v, m_o, m_i, m_f, s_u, n_q, n_kv, n_g = seg
    small = jnp.concatenate([m_i, m_f, n_g], axis=1)
    pad = jnp.zeros((d, N_PROJ_PAD - COL_SMALL - small.shape[1]), w_in.dtype)
    return jnp.concatenate([m_qk, n_q, m_v, m_o, s_u, n_kv, small, pad], axis=1).astype(BF16)


def _pad_cols(a, n):
    return jnp.concatenate([a, jnp.zeros(a.shape[:-1] + (n - a.shape[-1],), a.dtype)], axis=-1)


def kernel(x, c, positions, w_ada, b_ada, g_pre_mix, g_post_mix, g_pre_ffn, g_post_ffn, w_in, w_out, m_conv_w, m_conv_b, m_i_bias, m_f_bias, m_norm_g, s_lam_re, s_lam_im, s_log_dt, s_b_re, s_b_im, s_c_re, s_c_im, s_d, s_w_glu, s_b_glu, n_cmp_pos, n_cmp_w1, n_cmp_w2, f_w_up, f_conv_w, f_conv_b, f_w_down):
    batch, seq, d = x.shape
    assert batch == 1 and d == D_MODEL and seq % 1024 == 0
    depth = w_ada.shape[0]
    xs = x.reshape(seq, d)
    mod = _ada_mod(c.reshape(d, 1), w_ada, b_ada)
    cos_f, sin_f = _rope_tables(positions.reshape(seq, 1))
    row = lambda a: a.reshape(1, -1)
    for l in range(depth):
        sh1, sc1, gt1, sh2, sc2, gt2 = [mod[l, :, k * d:(k + 1) * d] for k in range(6)]
        proj = _inproj(xs, row(g_pre_mix[l]), sc1, sh1, _proj_weight(w_in[l]))
        gate_bias = _pad_cols(jnp.concatenate([m_i_bias[l], m_f_bias[l]]).reshape(1, -1), 128)
        y_m = _mlstm(proj, m_conv_w[l], row(m_conv_b[l]), gate_bias, row(m_norm_g[l]))
        ops = _s5_prep(s_lam_re[l], s_lam_im[l], s_log_dt[l], s_b_re[l], s_b_im[l], s_c_re[l], s_c_im[l])
        y_s = _s5(proj, ops, row(s_d[l]), s_w_glu[l].astype(BF16), row(s_b_glu[l]))
        y_n = _nsa(proj, cos_f, sin_f, n_cmp_pos[l], n_cmp_w1[l], n_cmp_w2[l])
        xs = _outproj(y_m, y_s, y_n, w_out[l].astype(BF16), xs, row(g_post_mix[l]), gt1)
        wa = _pad_cols(f_w_up[l][:, :D_FF], D_FF_PAD).astype(BF16)
        wu = _pad_cols(f_w_up[l][:, D_FF:], D_FF_PAD).astype(BF16)
        wd = jnp.concatenate([f_w_down[l], jnp.zeros((D_FF_PAD - D_FF, d), F32)], axis=0).astype(BF16)
        xs = _ffn(xs, row(g_pre_ffn[l]), sc2, sh2, wa, wu, _pad_cols(f_conv_w[l], D_FF_PAD),
                  _pad_cols(row(f_conv_b[l]), D_FF_PAD), wd, row(g_post_ffn[l]), gt2)
    return xs.reshape(batch, seq, d)
```

```python
import functools

import numpy as np
import jax
import jax.numpy as jnp
from jax import lax
from jax.experimental import pallas as pl
from jax.experimental.pallas import tpu as pltpu

F32 = jnp.float32
BF16 = jnp.bfloat16

D_MODEL = 2048
EPS = 1e-6
M_WIDTH = 512
M_HEADS = 4
M_HEAD_DIM = 128
M_CHUNK = 128
M_CONV = 4
S_WIDTH = 512
S_GROUP = 16
S_GROUPS = 32
S_STATE = 64
S_T = 16
S_OCT = 8
N_WIDTH = 1024
N_HEAD_DIM = 128
N_HEADS = 8
N_KV_HEADS = 2
N_REP = N_HEADS // N_KV_HEADS
CMP_BLOCK = 32
CMP_STRIDE = 16
SLC_BLOCK = 64
SLC_TOPK = 16
WINDOW = 512
Q_BLOCK = 128
ROPE_THETA = 10000.0
NEG_INF = -1e30
FORCED_SCORE = 1e9
SLC_TILE = 512
V_EXTRA = 16
D_FF = 5504
D_FF_PAD = 5632
FFN_TILE = 512
COL_MQK, COL_NQ, COL_MV, COL_MO, COL_SU, COL_NKV, COL_SMALL = 0, 1024, 2048, 2560, 3072, 3584, 5120
N_PROJ_PAD = 5376
PROJ_TILE = 768
SMALL_I, SMALL_F, SMALL_G = 0, 4, 8

VMEM_LIMIT_BYTES = 58 * 1024 * 1024


def _params(*sem):
    return pltpu.CompilerParams(dimension_semantics=sem, vmem_limit_bytes=VMEM_LIMIT_BYTES)


def _gelu(x):
    return 0.5 * x * (1.0 + jnp.tanh(0.7978845608028654 * (x + 0.044715 * (x * x * x))))


def _rms(x):
    return lax.rsqrt(jnp.mean(x * x, axis=-1, keepdims=True) + EPS)


def _dot(a, b):
    return jnp.dot(a, b, preferred_element_type=F32)


def _dot_nt(a, b):
    return lax.dot_general(a, b, (((1,), (1,)), ((), ())), preferred_element_type=F32)


def _ada_kernel(c_ref, w_ref, b_ref, o_ref):
    c = c_ref[...]
    cs = c * jax.nn.sigmoid(c)
    o_ref[...] = jnp.sum(w_ref[...] * cs, axis=0, keepdims=True) + b_ref[...]


def _ada_mod(c_col, w_ada, b_ada):
    depth, d, n = w_ada.shape
    tn = 1024
    return pl.pallas_call(
        _ada_kernel,
        grid=(depth, n // tn),
        in_specs=[pl.BlockSpec((d, 1), lambda l, j: (0, 0)),
                  pl.BlockSpec((None, d, tn), lambda l, j: (l, 0, j)),
                  pl.BlockSpec((None, 1, tn), lambda l, j: (l, 0, j))],
        out_specs=pl.BlockSpec((None, 1, tn), lambda l, j: (l, 0, j)),
        out_shape=jax.ShapeDtypeStruct((depth, 1, n), F32),
        compiler_params=_params("arbitrary", "arbitrary"),
    )(c_col, w_ada, b_ada.reshape(depth, 1, n))


def _inproj_kernel(x_ref, g_ref, sc_ref, sh_ref, w_ref, o_ref, h_ref):
    @pl.when(pl.program_id(1) == 0)
    def _():
        x = x_ref[...]
        h = (x * _rms(x)) * g_ref[...]
        h_ref[...] = (h * (1.0 + sc_ref[...]) + sh_ref[...]).astype(BF16)

    o_ref[...] = _dot(h_ref[...], w_ref[...])


def _inproj(x, g, sc, sh, w):
    s, d = x.shape
    n = w.shape[1]
    tm, tn = min(s, 1024), PROJ_TILE
    row = pl.BlockSpec((1, d), lambda i, j: (0, 0))
    return pl.pallas_call(
        _inproj_kernel,
        grid=(s // tm, n // tn),
        in_specs=[pl.BlockSpec((tm, d), lambda i, j: (i, 0)), row, row, row,
                  pl.BlockSpec((d, tn), lambda i, j: (0, j))],
        out_specs=pl.BlockSpec((tm, tn), lambda i, j: (i, j)),
        out_shape=jax.ShapeDtypeStruct((s, n), F32),
        scratch_shapes=[pltpu.VMEM((tm, d), BF16)],
        compiler_params=_params("arbitrary", "arbitrary"),
    )(x, g, sc, sh, w)


def _mlstm_kernel(qk_ref, v_ref, o_ref, sm_ref, cw_ref, cb_ref, gb_ref, ng_ref, y_ref,
                  tail_ref, c_ref, n_ref, m_ref):
    L, Dh = M_CHUNK, M_HEAD_DIM

    @pl.when(pl.program_id(0) == 0)
    def _():
        tail_ref[...] = jnp.zeros_like(tail_ref)
        c_ref[...] = jnp.zeros_like(c_ref)
        n_ref[...] = jnp.zeros_like(n_ref)
        m_ref[...] = jnp.zeros_like(m_ref)

    raw = qk_ref[...]
    tail = tail_ref[...]
    row = lax.broadcasted_iota(jnp.int32, raw.shape, 0)

    def shifted(d):
        r = pltpu.roll(raw, d, axis=0)
        for k in range(d):
            r = jnp.where(row == k, tail[8 - d + k:8 - d + k + 1, :], r)
        return r

    cw = cw_ref[...]
    pre = (cw[3:4] * raw + cw[2:3] * shifted(1) + cw[1:2] * shifted(2) + cw[0:1] * shifted(3)
           + cb_ref[...])
    tail_ref[...] = raw[L - 8:L, :]
    qk = pre * jax.nn.sigmoid(pre)

    sm = sm_ref[...] + gb_ref[...]
    lf = jnp.minimum(sm, 0.0) - jnp.log1p(jnp.exp(-jnp.abs(sm)))
    rowg = lax.broadcasted_iota(jnp.int32, lf.shape, 0)
    g = lf
    for s in (1, 2, 4, 8, 16, 32, 64):
        g = g + jnp.where(rowg >= s, pltpu.roll(g, s, axis=0), 0.0)
    smT = sm.T
    gT = g.T
    ri = lax.broadcasted_iota(jnp.int32, (L, L), 0)
    ci = lax.broadcasted_iota(jnp.int32, (L, L), 1)
    causal = ri >= ci

    H = range(M_HEADS)
    hs = [slice(h * Dh, (h + 1) * Dh) for h in H]
    q = [qk[:, hs[h]] for h in H]
    k = [qk[:, M_WIDTH + h * Dh:M_WIDTH + (h + 1) * Dh] * (Dh ** -0.5) for h in H]
    v = [v_ref[:, hs[h]] for h in H]
    i_c = [sm[:, SMALL_I + h:SMALL_I + h + 1] for h in H]
    g_c = [g[:, SMALL_F + h:SMALL_F + h + 1] for h in H]
    i_r = [smT[SMALL_I + h:SMALL_I + h + 1, :] for h in H]
    g_r = [gT[SMALL_F + h:SMALL_F + h + 1, :] for h in H]
    g_last = [g_r[h][:, L - 1:L] for h in H]
    m_prev = [m_ref[h:h + 1, 0:1] for h in H]
    c_prev = [c_ref[h] for h in H]
    n_prev = [n_ref[h:h + 1, :] for h in H]
    qb = [q[h].astype(BF16) for h in H]
    kb = [k[h].astype(BF16) for h in H]
    vb = [v[h].astype(BF16) for h in H]
    a_max = [jnp.max(g_last[h] - g_r[h] + i_r[h], axis=1, keepdims=True) for h in H]
    dlog = [jnp.where(causal, g_c[h] - g_r[h] + i_r[h], -jnp.inf) for h in H]
    d_max = [jnp.max(dlog[h], axis=1, keepdims=True) for h in H]
    qn = [jnp.sum(q[h] * n_prev[h], axis=1, keepdims=True) for h in H]
    s_qk = [_dot_nt(qb[h], kb[h]) for h in H]
    q_c = [_dot(qb[h], c_prev[h].astype(BF16)) for h in H]
    w_c = [jnp.exp(g_last[h] - g_c[h] + i_c[h] - a_max[h]) for h in H]
    c_loc = [_dot(k[h].T.astype(BF16), (w_c[h] * v[h]).astype(BF16)) for h in H]
    n_loc = [jnp.sum(w_c[h] * k[h], axis=0, keepdims=True) for h in H]
    m_new = [jnp.maximum(g_last[h] + m_prev[h], a_max[h]) for h in H]
    s_old = [jnp.exp(g_last[h] + m_prev[h] - m_new[h]) for h in H]
    s_new = [jnp.exp(a_max[h] - m_new[h]) for h in H]
    for h in H:
        c_ref[h] = s_old[h] * c_prev[h] + s_new[h] * c_loc[h]
        n_ref[h:h + 1, :] = s_old[h] * n_prev[h] + s_new[h] * n_loc[h]
        m_ref[h:h + 1, :] = jnp.broadcast_to(m_new[h], (1, Dh))
    m_inter = [g_c[h] + m_prev[h] for h in H]
    m_t = [jnp.maximum(m_inter[h], d_max[h]) for h in H]
    sw = [s_qk[h] * jnp.exp(dlog[h] - m_t[h]) for h in H]
    inter = [jnp.exp(m_inter[h] - m_t[h]) for h in H]
    num = [_dot(sw[h].astype(BF16), vb[h]) + inter[h] * q_c[h] for h in H]
    den = [jnp.sum(sw[h], axis=1, keepdims=True) + inter[h] * qn[h] for h in H]
    hh = [num[h] / jnp.maximum(jnp.abs(den[h]), jnp.exp(-m_t[h])) for h in H]
    rms = [_rms(hh[h]) for h in H]
    for h in H:
        hn = (hh[h] * rms[h]) * ng_ref[:, hs[h]]
        y_ref[:, hs[h]] = (jax.nn.sigmoid(o_ref[:, hs[h]]) * hn).astype(BF16)


def _mlstm(proj, conv_w, conv_b, gate_bias, norm_g):
    s = proj.shape[0]
    L = M_CHUNK
    full = lambda shape: pl.BlockSpec(shape, lambda c: (0,) * len(shape))
    return pl.pallas_call(
        _mlstm_kernel,
        grid=(s // L,),
        in_specs=[pl.BlockSpec((L, 2 * M_WIDTH), lambda c: (c, COL_MQK // (2 * M_WIDTH))),
                  pl.BlockSpec((L, M_WIDTH), lambda c: (c, COL_MV // M_WIDTH)),
                  pl.BlockSpec((L, M_WIDTH), lambda c: (c, COL_MO // M_WIDTH)),
                  pl.BlockSpec((L, 128), lambda c: (c, COL_SMALL // 128)),
                  full((M_CONV, 2 * M_WIDTH)), full((1, 2 * M_WIDTH)), full((1, 128)), full((1, M_WIDTH))],
        out_specs=pl.BlockSpec((L, M_WIDTH), lambda c: (c, 0)),
        out_shape=jax.ShapeDtypeStruct((s, M_WIDTH), BF16),
        scratch_shapes=[pltpu.VMEM((8, 2 * M_WIDTH), F32),
                        pltpu.VMEM((M_HEADS, M_HEAD_DIM, M_HEAD_DIM), F32),
                        pltpu.VMEM((8, M_HEAD_DIM), F32),
                        pltpu.VMEM((8, M_HEAD_DIM), F32)],
        compiler_params=_params("arbitrary"),
    )(proj, proj, proj, proj, conv_w, conv_b, gate_bias, norm_g)


def _split3(a):
    hi = a.astype(BF16)
    r = a - hi.astype(F32)
    mid = r.astype(BF16)
    lo = (r - mid.astype(F32)).astype(BF16)
    return hi, mid, lo


def _dot_f32(a, b):
    ah, am, al = _split3(a)
    bh, bm, bl = _split3(b)
    small = _dot(ah, bl) + _dot(am, bm) + _dot(al, bh)
    mid = _dot(ah, bm) + _dot(am, bh)
    return _dot(ah, bh) + (mid + small)


def _s5_prep_kernel(lrr_ref, lir_ref, lrc_ref, lic_ref, ldt_ref, btr_ref, bti_ref, ctr_ref, cti_ref, place_ref,
                    m_ref, pr_ref, pi_ref, rr_ref, ri_ref, ar_ref, ai_ref):
    T = S_T
    dt = jnp.exp(ldt_ref[...])
    lr, li = lrr_ref[...], lir_ref[...]
    mag = jnp.exp(lr * dt)
    a_re = mag * jnp.cos(li * dt)
    a_im = mag * jnp.sin(li * dt)
    den = lr * lr + li * li
    nr = a_re - 1.0
    z_re = (nr * lr + a_im * li) / den
    z_im = (a_im * lr - nr * li) / den
    btr, bti = btr_ref[...], bti_ref[...]
    bb_re = z_re * btr - z_im * bti
    bb_im = z_re * bti + z_im * btr

    def pow_row(e):
        mg = jnp.exp((lr * dt) * e)
        th = (li * dt) * e
        return mg * jnp.cos(th), mg * jnp.sin(th)

    jrow = lax.shift_right_logical(lax.broadcasted_iota(jnp.int32, btr.shape, 0), 4).astype(F32)
    p_re, p_im = pow_row(-jrow)
    left_re = bb_re * p_re - bb_im * p_im
    left_im = bb_re * p_im + bb_im * p_re
    q_re, q_im = pow_row((T - 1.0) - jrow)
    pr_ref[...] = bb_re * q_re - bb_im * q_im
    pi_ref[...] = bb_re * q_im + bb_im * q_re

    lrc, lic = lrc_ref[...], lic_ref[...]
    ctr, cti = ctr_ref[...], cti_ref[...]

    def pow_col(e):
        mg = jnp.exp((lrc * dt) * e)
        th = (lic * dt) * e
        return mg * jnp.cos(th), mg * jnp.sin(th)

    icol = lax.shift_right_logical(lax.broadcasted_iota(jnp.int32, ctr.shape, 1), 4).astype(F32)
    e_re, e_im = pow_col(icol)
    right_re = ctr * e_re - cti * e_im
    right_im = ctr * e_im + cti * e_re
    m = _dot_f32(left_re, right_re) - _dot_f32(left_im, right_im)
    mj = lax.shift_right_logical(lax.broadcasted_iota(jnp.int32, m.shape, 0), 4)
    mi = lax.shift_right_logical(lax.broadcasted_iota(jnp.int32, m.shape, 1), 4)
    m_ref[...] = _dot(jnp.where(mi >= mj, m, 0.0).astype(BF16), place_ref[...]).astype(BF16)
    f_re, f_im = pow_col(icol + 1.0)
    rr_ref[...] = ctr * f_re - cti * f_im
    ri_ref[...] = -(ctr * f_im + cti * f_re)
    mag_t = jnp.exp((lr * dt) * T)
    ar_ref[...] = mag_t * jnp.cos((li * dt) * T)
    ai_ref[...] = mag_t * jnp.sin((li * dt) * T)


def _s5_prep(lam_re, lam_im, log_dt, b_re, b_im, c_re, c_im):
    G, N, C, T = S_GROUPS, S_STATE, S_GROUP, S_T
    TC = T * C
    bt = lambda b: jnp.tile(jnp.swapaxes(b, 1, 2), (1, T, 1))
    ct = lambda c: jnp.tile(jnp.swapaxes(c, 1, 2), (1, 1, T))
    blk = lambda a, b: pl.BlockSpec((None, a, b), lambda g: (g, 0, 0))
    O, NO = S_OCT, G // S_OCT
    W = T * O * C
    col = np.arange(W)
    src = (col // (O * C)) * C + col % C
    place = np.stack([(np.arange(TC)[:, None] == src[None, :]) & ((col // C) % O == g)[None, :]
                      for g in range(O)]).astype(np.float32)
    outs = pl.pallas_call(
        _s5_prep_kernel,
        grid=(G,),
        in_specs=[blk(1, N), blk(1, N), blk(N, 1), blk(N, 1), blk(1, 1),
                  blk(TC, N), blk(TC, N), blk(N, TC), blk(N, TC),
                  pl.BlockSpec((None, TC, W), lambda g: (g % O, 0, 0))],
        out_specs=[blk(TC, W), blk(TC, N), blk(TC, N), blk(N, TC), blk(N, TC), blk(1, N), blk(1, N)],
        out_shape=[jax.ShapeDtypeStruct((G, TC, W), BF16),
                   jax.ShapeDtypeStruct((G, TC, N), F32), jax.ShapeDtypeStruct((G, TC, N), F32),
                   jax.ShapeDtypeStruct((G, N, TC), F32), jax.ShapeDtypeStruct((G, N, TC), F32),
                   jax.ShapeDtypeStruct((G, 1, N), F32), jax.ShapeDtypeStruct((G, 1, N), F32)],
        compiler_params=_params("arbitrary"),
    )(lam_re.reshape(G, 1, N), lam_im.reshape(G, 1, N), lam_re.reshape(G, N, 1), lam_im.reshape(G, N, 1),
      log_dt.reshape(G, 1, 1), bt(b_re), bt(b_im), ct(c_re), ct(c_im), jnp.asarray(place, BF16))
    m, pr, pi, rr, ri, ar, ai = outs

    m_oct = m.reshape(NO, O, T, C, W).transpose(0, 2, 1, 3, 4).reshape(NO, W, W)

    def p_oct(p):
        p5 = p.astype(BF16).reshape(NO, O, T, C, N)
        out = jnp.zeros((NO, T, O, C, O, N), BF16)
        for g in range(O):
            out = out.at[:, :, g, :, g, :].set(p5[:, g])
        return out.reshape(NO, W, O * N)

    def r_oct(r):
        r5 = r.astype(BF16).reshape(NO, O, N, T, C)
        out = jnp.zeros((NO, O, N, T, O, C), BF16)
        for g in range(O):
            out = out.at[:, g, :, :, g, :].set(r5[:, g])
        return out.reshape(NO, O * N, W)

    oct_row = lambda a: a.reshape(NO, 1, O * N)
    return (m_oct, p_oct(pr), p_oct(pi), r_oct(rr), r_oct(ri), oct_row(ar), oct_row(ai))


def _s5_kernel(u_ref, m_ref, pr_ref, pi_ref, rr_ref, ri_ref, ar_ref, ai_ref, y_ref,
               xr_s, xi_s, er_s, ei_s, st_s):
    T = S_T
    nch = u_ref.shape[0] // T

    @pl.when(pl.program_id(1) == 0)
    def _():
        st_s[...] = jnp.zeros_like(st_s)

    u = jnp.concatenate([u_ref[pl.ds(j, nch, stride=T), :] for j in range(T)], axis=1).astype(BF16)
    y_loc = _dot(u, m_ref[...])
    xr_s[...] = _dot(u, pr_ref[...])
    xi_s[...] = _dot(u, pi_ref[...])
    a_re, a_im = ar_ref[...], ai_ref[...]

    def body(b, carry):
        e_re, e_im = carry
        base = pl.multiple_of(b * 8, 8)
        xr = xr_s[pl.ds(base, 8), :]
        xi = xi_s[pl.ds(base, 8), :]
        for k in range(8):
            er_s[pl.ds(base + k, 1), :] = e_re
            ei_s[pl.ds(base + k, 1), :] = e_im
            n_re = a_re * e_re - a_im * e_im + xr[k:k + 1]
            n_im = a_re * e_im + a_im * e_re + xi[k:k + 1]
            e_re, e_im = n_re, n_im
        return e_re, e_im

    e_re, e_im = lax.fori_loop(0, nch // 8, body, (st_s[0:1, :], st_s[1:2, :]))
    st_s[0:1, :] = e_re
    st_s[1:2, :] = e_im
    y = (y_loc + _dot(er_s[...].astype(BF16), rr_ref[...]) + _dot(ei_s[...].astype(BF16), ri_ref[...]))
    lanes = y_ref.shape[1]
    for i in range(T):
        y_ref[pl.ds(i, nch, stride=T), :] = y[:, i * lanes:(i + 1) * lanes]


def _s5_scan(proj, ops):
    m, pr, pi, rr, ri, ar, ai = ops
    s = proj.shape[0]
    no, w, ns = pr.shape
    lanes = S_OCT * S_GROUP
    ts = min(s, 4096)
    nch = ts // S_T
    op = lambda a, b: pl.BlockSpec((None, a, b), lambda o, r: (o, 0, 0))
    return pl.pallas_call(
        _s5_kernel,
        grid=(no, s // ts),
        in_specs=[pl.BlockSpec((ts, lanes), lambda o, r: (r, COL_SU // lanes + o)),
                  op(w, w), op(w, ns), op(w, ns), op(ns, w), op(ns, w), op(1, ns), op(1, ns)],
        out_specs=pl.BlockSpec((None, ts, lanes), lambda o, r: (o, r, 0)),
        out_shape=jax.ShapeDtypeStruct((no, s, lanes), F32),
        scratch_shapes=[pltpu.VMEM((nch, ns), F32) for _ in range(4)] + [pltpu.VMEM((8, ns), F32)],
        compiler_params=_params("arbitrary", "arbitrary"),
    )(proj, m, pr, pi, rr, ri, ar, ai)


def _s5_glu_kernel(y_ref, u_ref, d_ref, w_ref, b_ref, o_ref):
    y = jnp.concatenate([y_ref[o] for o in range(y_ref.shape[0])], axis=1)
    y = _gelu(y + d_ref[...] * u_ref[...])
    z = _dot(y.astype(BF16), w_ref[...]) + b_ref[...]
    o_ref[...] = (y * jax.nn.sigmoid(z)).astype(BF16)


def _s5_glu(y, proj, d_skip, w_glu, b_glu):
    no, s, lanes = y.shape
    w = no * lanes
    tm = 1024
    row = pl.BlockSpec((1, w), lambda i: (0, 0))
    return pl.pallas_call(
        _s5_glu_kernel,
        grid=(s // tm,),
        in_specs=[pl.BlockSpec((no, tm, lanes), lambda i: (0, i, 0)),
                  pl.BlockSpec((tm, w), lambda i: (i, COL_SU // S_WIDTH)),
                  row, pl.BlockSpec((w, w), lambda i: (0, 0)), row],
        out_specs=pl.BlockSpec((tm, w), lambda i: (i, 0)),
        out_shape=jax.ShapeDtypeStruct((s, w), BF16),
        compiler_params=_params("arbitrary"),
    )(y, proj, d_skip, w_glu, b_glu)


def _s5(proj, ops, d_skip, w_glu, b_glu):
    return _s5_glu(_s5_scan(proj, ops), proj, d_skip, w_glu, b_glu)


def _rope_tab_kernel(pos_ref, inv_ref, sgn_ref, cos_ref, sin_ref):
    ang = pos_ref[...].astype(F32) * inv_ref[...]
    cos_ref[...] = jnp.cos(ang)
    sin_ref[...] = jnp.sin(ang) * sgn_ref[...]


def _rope_tables(pos_col):
    s = pos_col.shape[0]
    half = N_HEAD_DIM // 2
    inv = (ROPE_THETA ** (-jnp.arange(half, dtype=F32) / half))
    inv = jnp.concatenate([inv, inv]).reshape(1, N_HEAD_DIM)
    sgn = jnp.concatenate([-jnp.ones((half,), F32), jnp.ones((half,), F32)]).reshape(1, N_HEAD_DIM)
    tm = min(s, 2048)
    row = pl.BlockSpec((1, N_HEAD_DIM), lambda i: (0, 0))
    tab = pl.BlockSpec((tm, N_HEAD_DIM), lambda i: (i, 0))
    return pl.pallas_call(
        _rope_tab_kernel,
        grid=(s // tm,),
        in_specs=[pl.BlockSpec((tm, 1), lambda i: (i, 0)), row, row],
        out_specs=[tab, tab],
        out_shape=[jax.ShapeDtypeStruct((s, N_HEAD_DIM), F32)] * 2,
        compiler_params=_params("arbitrary"),
    )(pos_col, inv, sgn)


def _col_reduce(x, op):
    while x.shape[0] > 8 and x.shape[0] % 64 == 0:
        x = op(x.reshape(8, x.shape[0] // 8, x.shape[1]), axis=0)
    return op(x, axis=0, keepdims=True)


def _colmax(x):
    return _col_reduce(x, jnp.max)


def _colsum(x):
    return _col_reduce(x, jnp.sum)


def _rope(x, cos_f, sin_f):
    return x * cos_f + pltpu.roll(x, N_HEAD_DIM // 2, axis=1) * sin_f


def _nsa_pre_kernel(*refs):
    kv = refs[0:6 * N_KV_HEADS]
    cos_ref, sin_ref = refs[12], refs[13]
    kcmp_o, vcmp_o, kslc_o, vslct_o, kwin_o, vwint_o = refs[14:]
    cos_f, sin_f = cos_ref[...], sin_ref[...]
    tm = cos_f.shape[0]
    key = pl.program_id(0) * tm + lax.broadcasted_iota(jnp.int32, (tm, N_HEAD_DIM), 0)
    blk = lax.shift_right_logical(key, 6) & (SLC_TILE // SLC_BLOCK - 1)
    onehot = jnp.where(lax.broadcasted_iota(jnp.int32, (tm, N_HEAD_DIM), 1) == blk, 1.0, 0.0).astype(BF16)
    ones_rows = jnp.where(lax.broadcasted_iota(jnp.int32, (V_EXTRA, tm), 0) == 0, 1.0, 0.0)
    pieces = [pl.ds(r, tm // CMP_STRIDE, stride=CMP_STRIDE) for r in range(CMP_STRIDE)]
    for kh in range(N_KV_HEADS):
        kcmp_o[kh] = jnp.concatenate(
            [_rope(kv[0 * N_KV_HEADS + kh][rows, :], cos_ref[rows, :], sin_ref[rows, :]) for rows in pieces], axis=1)
        vcmp_o[kh] = jnp.concatenate([kv[1 * N_KV_HEADS + kh][rows, :] for rows in pieces], axis=1)
        kslc_o[kh] = jnp.concatenate(
            [_rope(kv[2 * N_KV_HEADS + kh][...], cos_f, sin_f).astype(BF16), onehot], axis=1)
        vslct_o[kh] = jnp.concatenate([kv[3 * N_KV_HEADS + kh][...].T, ones_rows], axis=0).astype(BF16)
        kwin_o[kh] = _rope(kv[4 * N_KV_HEADS + kh][...], cos_f, sin_f).astype(BF16)
        vwint_o[kh] = jnp.concatenate([kv[5 * N_KV_HEADS + kh][...].T, ones_rows], axis=0).astype(BF16)


def _nsa_pre(proj, cos_f, sin_f):
    s = proj.shape[0]
    Hk, Dh = N_KV_HEADS, N_HEAD_DIM
    tm = 512
    kv_specs = [pl.BlockSpec((tm, Dh), functools.partial(lambda i, c: (i, c), c=COL_NKV // Dh + n))
                for n in range(6 * Hk)]
    tab = pl.BlockSpec((tm, Dh), lambda i: (i, 0))
    std = pl.BlockSpec((Hk, tm, Dh), lambda i: (0, i, 0))
    tr = pl.BlockSpec((Hk, Dh + V_EXTRA, tm), lambda i: (0, 0, i))
    ext = pl.BlockSpec((Hk, tm, 2 * Dh), lambda i: (0, i, 0))
    pcs = pl.BlockSpec((Hk, tm // CMP_STRIDE, CMP_STRIDE * Dh), lambda i: (0, i, 0))
    return pl.pallas_call(
        _nsa_pre_kernel,
        grid=(s // tm,),
        in_specs=kv_specs + [tab, tab],
        out_specs=[pcs, pcs, ext, tr, std, tr],
        out_shape=[jax.ShapeDtypeStruct((Hk, s // CMP_STRIDE, CMP_STRIDE * Dh), F32),
                   jax.ShapeDtypeStruct((Hk, s // CMP_STRIDE, CMP_STRIDE * Dh), F32),
                   jax.ShapeDtypeStruct((Hk, s, 2 * Dh), BF16), jax.ShapeDtypeStruct((Hk, Dh + V_EXTRA, s), BF16),
                   jax.ShapeDtypeStruct((Hk, s, Dh), BF16), jax.ShapeDtypeStruct((Hk, Dh + V_EXTRA, s), BF16)],
        compiler_params=_params("arbitrary"),
    )(*([proj] * (6 * Hk)), cos_f, sin_f)


def _cmp_kernel(x_ref, p1_ref, p2_ref, wt_ref, wb_ref, w2_ref, o_ref):
    x = x_ref[...]
    first = _dot((x + p1_ref[...]).astype(BF16), wt_ref[...])
    second = _dot((x + p2_ref[...]).astype(BF16), wb_ref[...])
    npieces = x.shape[0]
    pre = first + pltpu.roll(second, npieces - 1, axis=0)
    o_ref[...] = _dot(_gelu(pre).astype(BF16), w2_ref[...])


def _compress(xp, pos, w1, w2):
    hk, npieces, half = xp.shape
    dh = half // CMP_STRIDE
    p1 = pos[:CMP_STRIDE].reshape(1, half)
    p2 = pos[CMP_STRIDE:].reshape(1, half)
    full = lambda shape: pl.BlockSpec(shape, lambda h: (0,) * len(shape))
    return pl.pallas_call(
        _cmp_kernel,
        grid=(hk,),
        in_specs=[pl.BlockSpec((None, npieces, half), lambda h: (h, 0, 0)),
                  full((1, half)), full((1, half)), full((half, dh)), full((half, dh)), full((dh, dh))],
        out_specs=pl.BlockSpec((None, npieces, dh), lambda h: (h, 0, 0)),
        out_shape=jax.ShapeDtypeStruct((hk, npieces, dh), F32),
        compiler_params=_params("arbitrary"),
    )(xp, p1, p2, w1[:half].astype(BF16), w1[half:].astype(BF16), w2.astype(BF16))


def _nsa_attn_kernel(*refs, seq):
    gens = [_nsa_head(kh, refs, seq) for kh in range(N_KV_HEADS)]
    heads = [None] * len(gens)
    while any(h is None for h in heads):
        for i, gen in enumerate(gens):
            if heads[i] is None:
                try:
                    next(gen)
                except StopIteration as done:
                    heads[i] = done.value
    n_tiles = heads[0][0]

    def run_tiles(base, count):
        for i in range(count):
            stages = [head[1 + (i & 1)](base + i) for head in heads]
            for stage in range(len(stages[0])):
                for head_stages in stages:
                    head_stages[stage]()

    def slc_pair(k2, carry):
        run_tiles(2 * k2, 2)
        return carry

    lax.fori_loop(0, (n_tiles + 1) // 2, slc_pair, 0)
    for head in heads:
        head[3]()


def _nsa_head(kh, refs, seq):
    NW = WINDOW // Q_BLOCK + 1
    q_ref, sm_ref, cos_ref, sin_ref, cend_ref = refs[:5]
    kc_ref, vct_ref, ks_ref, vst_ref, ksd_ref, vsd_ref = [r.at[kh] for r in refs[5:11]]
    kw_refs = [r.at[kh] for r in refs[11:11 + NW]]
    vw_refs = [r.at[kh] for r in refs[11 + NW:11 + 2 * NW]]
    y_ref, gate_s = refs[11 + 2 * NW:13 + 2 * NW]
    (bias_s, rhs_s, sa_s, sb_s, pa_s, pb_s, aa_s, ab_s, m_s, acc_s, ocmp_s, owin_s
     ) = [r.at[kh] for r in refs[13 + 2 * NW:]]
    Dh, QB, G, TK = N_HEAD_DIM, Q_BLOCK, N_REP, SLC_TILE
    n_slc = seq // SLC_BLOCK
    n_cmp = seq // CMP_STRIDE
    ratio = SLC_BLOCK // CMP_STRIDE
    qb = pl.program_id(0)
    t0 = qb * QB
    c2 = (Dh ** -0.5) * 1.4426950408889634
    cos_f, sin_f = cos_ref[...], sin_ref[...]
    q = q_ref[:, kh * G * Dh:(kh + 1) * G * Dh]
    q_t = jnp.concatenate(
        [(_rope(q[:, g * Dh:(g + 1) * Dh], cos_f, sin_f) * c2).T.astype(BF16) for g in range(G)],
        axis=1)
    if kh == 0:
        gate_s[...] = jax.nn.sigmoid(sm_ref[...]).T
    t_lane = t0 + lax.broadcasted_iota(jnp.int32, (1, QB), 1)

    s_all = _dot(kc_ref[...], q_t)
    yield
    valid = cend_ref[...] <= t0
    any_valid = t_lane >= CMP_BLOCK - 1
    pk = jnp.zeros((n_cmp, QB), F32)
    p_parts = []
    for g in range(G):
        s_g = jnp.where(valid, s_all[:, g * QB:(g + 1) * QB], NEG_INF)
        e = jnp.exp2(s_g - _colmax(s_g))
        p = e * jnp.where(any_valid, 1.0 / _colsum(e), 0.0)
        pk = pk + p
        p_parts.append(p.astype(BF16))
    o_cmp = _dot(vct_ref[...], jnp.concatenate(p_parts, axis=1))
    yield

    s = _dot(jnp.concatenate([r[...] for r in kw_refs], axis=0), q_t)
    wpos = t0 - WINDOW + lax.broadcasted_iota(jnp.int32, (WINDOW + QB, QB), 0)
    wbias = jnp.where(wpos <= t_lane, jnp.where(wpos > t_lane - WINDOW, jnp.where(wpos >= 0, 0.0, NEG_INF),
                                                NEG_INF), NEG_INF)
    s = s + jnp.concatenate([wbias] * G, axis=1)
    p = jnp.exp2(s - _colmax(s))
    o_win = _dot(jnp.concatenate([r[...] for r in vw_refs], axis=1), p.astype(BF16))
    o_win = o_win[0:Dh] / o_win[Dh:Dh + 1]
    yield

    rhs_s[0:Dh, :] = q_t
    rhs_s[Dh:2 * Dh, :] = jnp.zeros((Dh, G * QB), BF16)

    slabs = [pk[r * n_slc:(r + 1) * n_slc] for r in range(ratio)]
    row_a = lax.broadcasted_iota(jnp.int32, (n_slc, QB), 0)
    prev_last = jnp.where(row_a == 0, 0.0, pltpu.roll(slabs[ratio - 1], 1, axis=0))
    mid = slabs[0]
    for r in range(1, ratio - 1):
        mid = mid + slabs[r]
    imp = (prev_last + 2.0 * mid) + slabs[ratio - 1]
    tb = lax.shift_right_logical(t_lane, 6)
    allowed = row_a <= tb
    forced = (row_a == 0) | (row_a == tb) | (row_a == tb - 1)
    score = jnp.where(forced, -jnp.inf, jnp.where(allowed, imp, -jnp.inf))
    row_f = row_a.astype(F32)
    for _ in range(max(min(SLC_TOPK, n_slc) - 3, 0)):
        best = _colmax(score)
        idx = _col_reduce(jnp.where(score == best, row_f, float(n_slc)), jnp.min)
        score = jnp.where(row_f == idx, -jnp.inf, score)
        yield
    bias_s[...] = jnp.where(score == -jnp.inf, jnp.where(row_a < 2 * qb, 0.0, NEG_INF), NEG_INF)

    blocks_per_tile = TK // SLC_BLOCK
    n_tiles = (t0 + TK - 1) // TK

    def scores(kt, buf):
        live = kt < n_tiles
        kt = jnp.maximum(jnp.minimum(kt, n_tiles - 1), 0)
        b8 = bias_s[pl.ds(pl.multiple_of(kt * blocks_per_tile, blocks_per_tile), blocks_per_tile), :]
        b8 = jnp.where(live, b8, NEG_INF)
        rows = jnp.concatenate([jnp.concatenate([b8] * G, axis=1),
                                jnp.zeros((16 - blocks_per_tile, G * QB), F32)], axis=0)
        rhs_s[Dh:Dh + 16, :] = rows.astype(BF16)
        buf[...] = _dot(ks_ref[pl.ds(pl.multiple_of(kt * TK, TK), TK), :], rhs_s[...])

    def softmax(buf, p_buf, a_buf):
        s = buf[...]
        m_old = m_s[...]
        m_new = jnp.maximum(m_old, _colmax(s))
        a_buf[...] = jnp.exp2(m_old - m_new)
        p_buf[...] = jnp.exp2(s - m_new).astype(BF16)
        m_s[...] = m_new

    def pv_acc(kt, p_buf, a_buf):
        kt = jnp.maximum(jnp.minimum(kt, n_tiles - 1), 0)
        acc_s[...] = a_buf[...] * acc_s[...] + _dot(vst_ref[:, pl.ds(pl.multiple_of(kt * TK, TK), TK)], p_buf[...])

    s = _dot(ksd_ref[:, 0:Dh], q_t)
    kpos = t0 + lax.broadcasted_iota(jnp.int32, (QB, QB), 0)
    s = s + jnp.concatenate([jnp.where(kpos <= t_lane, 0.0, NEG_INF)] * G, axis=1)
    m0 = _colmax(s)
    p = jnp.exp2(s - m0)
    m_s[...] = m0
    acc_s[...] = _dot(vsd_ref[...], p.astype(BF16))
    yield

    scores(0, sa_s)
    yield
    scores(1, sb_s)
    softmax(sa_s, pa_s, aa_s)
    yield

    ocmp_s[...] = o_cmp
    owin_s[...] = o_win

    def finish():
        o_slc = acc_s[0:Dh, :] / acc_s[Dh:Dh + 1, :]
        outs = []
        for g in range(G):
            base = SMALL_G + (kh * G + g) * 3
            sl = slice(g * QB, (g + 1) * QB)
            o = (gate_s[base:base + 1, :] * ocmp_s[:, sl] + gate_s[base + 1:base + 2, :] * o_slc[:, sl]
                 + gate_s[base + 2:base + 3, :] * owin_s[:, sl])
            outs.append(o.T)
        y_ref[:, kh * G * Dh:(kh + 1) * G * Dh] = jnp.concatenate(outs, axis=1).astype(BF16)

    def half(kt, s_cur, p_cur, a_cur, s_nxt, p_nxt, a_nxt):
        return (lambda: scores(kt + 2, s_cur), lambda: softmax(s_nxt, p_nxt, a_nxt),
                lambda: pv_acc(kt, p_cur, a_cur))

    even = (sa_s, pa_s, aa_s)
    odd = (sb_s, pb_s, ab_s)
    return n_tiles, lambda kt: half(kt, *even, *odd), lambda kt: half(kt, *odd, *even), finish


def _nsa_attn(proj, cos_f, sin_f, kc, vct, ks, vst, kw, vwt):
    s = proj.shape[0]
    Hk, Dh, QB, G = N_KV_HEADS, N_HEAD_DIM, Q_BLOCK, N_REP
    n_cmp = s // CMP_STRIDE
    nw = WINDOW // QB + 1
    ratio = SLC_BLOCK // CMP_STRIDE
    row = np.arange(n_cmp)
    j_cmp = (row % (n_cmp // ratio)) * ratio + row // (n_cmp // ratio)
    cmp_end = jnp.asarray((j_cmp * CMP_STRIDE + (CMP_BLOCK - 1))[:, None] - np.arange(QB)[None, :], jnp.int32)
    whole = lambda a, b: pl.BlockSpec((Hk, a, b), lambda qb: (0, 0, 0), pipeline_mode=pl.Buffered(1))
    tab = pl.BlockSpec((QB, Dh), lambda qb: (qb, 0))
    kw_specs = [pl.BlockSpec((Hk, QB, Dh), functools.partial(
        lambda qb, w: (0, jnp.maximum(qb - (nw - 1) + w, 0), 0), w=w)) for w in range(nw)]
    vw_specs = [pl.BlockSpec((Hk, Dh + V_EXTRA, QB), functools.partial(
        lambda qb, w: (0, 0, jnp.maximum(qb - (nw - 1) + w, 0)), w=w)) for w in range(nw)]
    lanes = G * QB
    return pl.pallas_call(
        functools.partial(_nsa_attn_kernel, seq=s),
        grid=(s // QB,),
        in_specs=[pl.BlockSpec((QB, N_WIDTH), lambda qb: (qb, COL_NQ // N_WIDTH)),
                  pl.BlockSpec((QB, 128), lambda qb: (qb, COL_SMALL // 128)),
                  tab, tab, pl.BlockSpec((n_cmp, QB), lambda qb: (0, 0), pipeline_mode=pl.Buffered(1)),
                  whole(n_cmp, Dh), whole(Dh, n_cmp), whole(s, 2 * Dh), whole(Dh + V_EXTRA, s),
                  pl.BlockSpec((Hk, QB, 2 * Dh), lambda qb: (0, qb, 0)),
                  pl.BlockSpec((Hk, Dh + V_EXTRA, QB), lambda qb: (0, 0, qb))] + kw_specs + vw_specs,
        out_specs=pl.BlockSpec((QB, N_WIDTH), lambda qb: (qb, 0)),
        out_shape=jax.ShapeDtypeStruct((s, N_WIDTH), BF16),
        scratch_shapes=[pltpu.VMEM((128, QB), F32),
                        pltpu.VMEM((Hk, s // SLC_BLOCK, QB), F32),
                        pltpu.VMEM((Hk, 2 * Dh, lanes), BF16),
                        pltpu.VMEM((Hk, SLC_TILE, lanes), F32),
                        pltpu.VMEM((Hk, SLC_TILE, lanes), F32),
                        pltpu.VMEM((Hk, SLC_TILE, lanes), BF16),
                        pltpu.VMEM((Hk, SLC_TILE, lanes), BF16),
                        pltpu.VMEM((Hk, 1, lanes), F32),
                        pltpu.VMEM((Hk, 1, lanes), F32),
                        pltpu.VMEM((Hk, 1, lanes), F32),
                        pltpu.VMEM((Hk, Dh + V_EXTRA, lanes), F32),
                        pltpu.VMEM((Hk, Dh, lanes), F32),
                        pltpu.VMEM((Hk, Dh, lanes), F32)],
        compiler_params=_params("arbitrary"),
    )(proj, proj, cos_f, sin_f, cmp_end, kc, vct, ks, vst, ks, vst, *([kw] * nw), *([vwt] * nw))


def _slab_major(a):
    hk, n, dh = a.shape
    ratio = SLC_BLOCK // CMP_STRIDE
    return a.reshape(hk, n // ratio, ratio, dh).transpose(0, 2, 1, 3).reshape(hk, n, dh)


def _nsa(proj, cos_f, sin_f, cmp_pos, cmp_w1, cmp_w2):
    kcmp, vcmp, ks, vst, kw, vwt = _nsa_pre(proj, cos_f, sin_f)
    kc = _slab_major(_compress(kcmp, cmp_pos[0], cmp_w1[0], cmp_w2[0])).astype(BF16)
    vc = _slab_major(_compress(vcmp, cmp_pos[1], cmp_w1[1], cmp_w2[1])).astype(BF16)
    return _nsa_attn(proj, cos_f, sin_f, kc, jnp.swapaxes(vc, 1, 2), ks, vst, kw, vwt)


def _outproj_kernel(ym_ref, ys_ref, yn_ref, w_ref, x_ref, g_ref, gt_ref, o_ref):
    y = (_dot(ym_ref[...], w_ref[0:M_WIDTH, :]) + _dot(ys_ref[...], w_ref[M_WIDTH:M_WIDTH + S_WIDTH, :])
         + _dot(yn_ref[...], w_ref[M_WIDTH + S_WIDTH:, :]))
    o_ref[...] = x_ref[...] + gt_ref[...] * ((y * _rms(y)) * g_ref[...])


def _outproj(ym, ys, yn, w, x, g, gt):
    s, d = x.shape
    tm = 512
    row = pl.BlockSpec((1, d), lambda i: (0, 0))
    rows = lambda width: pl.BlockSpec((tm, width), lambda i: (i, 0))
    return pl.pallas_call(
        _outproj_kernel,
        grid=(s // tm,),
        in_specs=[rows(M_WIDTH), rows(S_WIDTH), rows(N_WIDTH), pl.BlockSpec((d, d), lambda i: (0, 0)),
                  rows(d), row, row],
        out_specs=rows(d),
        out_shape=jax.ShapeDtypeStruct((s, d), F32),
        compiler_params=_params("arbitrary"),
    )(ym, ys, yn, w, x, g, gt)


def _ffn_kernel(x_ref, g_ref, sc_ref, sh_ref, wa_ref, wu_ref, cw_ref, cb_ref, wd_ref, gp_ref, gt_ref,
                o_ref, h_ref, acc_ref, carry_ref):
    i = pl.program_id(0)
    j = pl.program_id(1)
    tm = x_ref.shape[0]

    @pl.when(j == 0)
    def _():
        x = x_ref[...]
        h = (x * _rms(x)) * g_ref[...]
        h_ref[...] = (h * (1.0 + sc_ref[...]) + sh_ref[...]).astype(BF16)
        acc_ref[...] = jnp.zeros_like(acc_ref)

    @pl.when(i == 0)
    def _():
        carry_ref[j] = jnp.zeros(carry_ref.shape[1:], F32)

    h = h_ref[...]
    a = _dot(h, wa_ref[...])
    u = _dot(h, wu_ref[...])
    prev = carry_ref[j]
    carry_ref[j] = a[tm - 8:tm, :]
    row = lax.broadcasted_iota(jnp.int32, a.shape, 0)
    a1 = jnp.where(row == 0, prev[7:8, :], pltpu.roll(a, 1, axis=0))
    a2 = jnp.where(row == 0, prev[6:7, :], jnp.where(row == 1, prev[7:8, :], pltpu.roll(a, 2, axis=0)))
    cw = cw_ref[...]
    conv = cw[0:1] * a2 + cw[1:2] * a1 + cw[2:3] * a + cb_ref[...]
    acc_ref[...] += _dot((_gelu(conv) * u).astype(BF16), wd_ref[...])

    @pl.when(j == pl.num_programs(1) - 1)
    def _():
        y = acc_ref[...]
        o_ref[...] = x_ref[...] + gt_ref[...] * ((y * _rms(y)) * gp_ref[...])


def _ffn(x, g, sc, sh, wa, wu, cw, cb, wd, gp, gt):
    s, d = x.shape
    fp = wa.shape[1]
    tm, tf = 512, FFN_TILE
    nj = fp // tf
    row = pl.BlockSpec((1, d), lambda i, j: (0, 0))
    return pl.pallas_call(
        _ffn_kernel,
        grid=(s // tm, nj),
        in_specs=[pl.BlockSpec((tm, d), lambda i, j: (i, 0)), row, row, row,
                  pl.BlockSpec((d, tf), lambda i, j: (0, j)),
                  pl.BlockSpec((d, tf), lambda i, j: (0, j)),
                  pl.BlockSpec((3, tf), lambda i, j: (0, j)),
                  pl.BlockSpec((1, tf), lambda i, j: (0, j)),
                  pl.BlockSpec((tf, d), lambda i, j: (j, 0)),
                  row, row],
        out_specs=pl.BlockSpec((tm, d), lambda i, j: (i, 0)),
        out_shape=jax.ShapeDtypeStruct((s, d), F32),
        scratch_shapes=[pltpu.VMEM((tm, d), BF16), pltpu.VMEM((tm, d), F32), pltpu.VMEM((nj, 8, tf), F32)],
        compiler_params=_params("arbitrary", "arbitrary"),
    )(x, g, sc, sh, wa, wu, cw, cb, wd, gp, gt)


def _proj_weight(w_in):
    d = w_in.shape[0]
    offs = np.cumsum([0, 2 * M_WIDTH, M_WIDTH, M_WIDTH, M_HEADS, M_HEADS, S_WIDTH, N_WIDTH,
                      6 * N_KV_HEADS * N_HEAD_DIM, 3 * N_HEADS])
    w_in = w_in.astype(BF16)
    seg = [w_in[:, offs[k]:offs[k + 1]] for k in range(9)]
    m_qk, m_v, m_o, m_i, m_f, s_u, n_q, n_kv, n_g = seg
    small = jnp.concatenate([m_i, m_f, n_g], axis=1)
    pad = jnp.zeros((d, N_PROJ_PAD - COL_SMALL - small.shape[1]), BF16)
    return jnp.concatenate([m_qk, n_q, m_v, m_o, s_u, n_kv, small, pad], axis=1)


def _pad_cols(a, n):
    return jnp.concatenate([a, jnp.zeros(a.shape[:-1] + (n - a.shape[-1],), a.dtype)], axis=-1)


def kernel(x, c, positions, w_ada, b_ada, g_pre_mix, g_post_mix, g_pre_ffn, g_post_ffn, w_in, w_out, m_conv_w, m_conv_b, m_i_bias, m_f_bias, m_norm_g, s_lam_re, s_lam_im, s_log_dt, s_b_re, s_b_im, s_c_re, s_c_im, s_d, s_w_glu, s_b_glu, n_cmp_pos, n_cmp_w1, n_cmp_w2, f_w_up, f_conv_w, f_conv_b, f_w_down):
    batch, seq, d = x.shape
    assert batch == 1 and d == D_MODEL and seq % 1024 == 0
    depth = w_ada.shape[0]
    xs = x.reshape(seq, d)
    mod = _ada_mod(c.reshape(d, 1), w_ada, b_ada)
    cos_f, sin_f = _rope_tables(positions.reshape(seq, 1))
    row = lambda a: a.reshape(1, -1)
    for l in range(depth):
        sh1, sc1, gt1, sh2, sc2, gt2 = [mod[l, :, k * d:(k + 1) * d] for k in range(6)]
        proj = _inproj(xs, row(g_pre_mix[l]), sc1, sh1, _proj_weight(w_in[l]))
        gate_bias = _pad_cols(jnp.concatenate([m_i_bias[l], m_f_bias[l]]).reshape(1, -1), 128)
        y_m = _mlstm(proj, m_conv_w[l], row(m_conv_b[l]), gate_bias, row(m_norm_g[l]))
        ops = _s5_prep(s_lam_re[l], s_lam_im[l], s_log_dt[l], s_b_re[l], s_b_im[l], s_c_re[l], s_c_im[l])
        y_s = _s5(proj, ops, row(s_d[l]), s_w_glu[l].astype(BF16), row(s_b_glu[l]))
        y_n = _nsa(proj, cos_f, sin_f, n_cmp_pos[l], n_cmp_w1[l], n_cmp_w2[l])
        xs = _outproj(y_m, y_s, y_n, w_out[l].astype(BF16), xs, row(g_post_mix[l]), gt1)
        w_up = f_w_up[l].astype(BF16)
        wa = _pad_cols(w_up[:, :D_FF], D_FF_PAD)
        wu = _pad_cols(w_up[:, D_FF:], D_FF_PAD)
        wd = jnp.concatenate([f_w_down[l].astype(BF16), jnp.zeros((D_FF_PAD - D_FF, d), BF16)], axis=0)
        xs = _ffn(xs, row(g_pre_ffn[l]), sc2, sh2, wa, wu, _pad_cols(f_conv_w[l], D_FF_PAD),
                  _pad_cols(row(f_conv_b[l]), D_FF_PAD), wd, row(g_post_ffn[l]), gt2)
    return xs.reshape(batch, seq, d)
```

```python
import functools

import numpy as np
import jax
import jax.numpy as jnp
from jax import lax
from jax.experimental import pallas as pl
from jax.experimental.pallas import tpu as pltpu

F32 = jnp.float32
BF16 = jnp.bfloat16

D_MODEL = 2048
EPS = 1e-6
M_WIDTH = 512
M_HEADS = 4
M_HEAD_DIM = 128
M_CHUNK = 128
M_CONV = 4
S_WIDTH = 512
S_GROUP = 16
S_GROUPS = 32
S_STATE = 64
S_T = 16
S_OCT = 8
N_WIDTH = 1024
N_HEAD_DIM = 128
N_HEADS = 8
N_KV_HEADS = 2
N_REP = N_HEADS // N_KV_HEADS
CMP_BLOCK = 32
CMP_STRIDE = 16
SLC_BLOCK = 64
SLC_TOPK = 16
WINDOW = 512
Q_BLOCK = 128
ROPE_THETA = 10000.0
NEG_INF = -1e30
FORCED_SCORE = 1e9
SLC_TILE = 512
V_EXTRA = 16
D_FF = 5504
D_FF_PAD = 5632
FFN_TILE = 512
COL_MQK, COL_NQ, COL_MV, COL_MO, COL_SU, COL_NKV, COL_SMALL = 0, 1024, 2048, 2560, 3072, 3584, 5120
N_PROJ_PAD = 5376
PROJ_TILE = 768
SMALL_I, SMALL_F, SMALL_G = 0, 4, 8

VMEM_LIMIT_BYTES = 58 * 1024 * 1024


def _params(*sem):
    return pltpu.CompilerParams(dimension_semantics=sem, vmem_limit_bytes=VMEM_LIMIT_BYTES)


def _gelu(x):
    return 0.5 * x * (1.0 + jnp.tanh(0.7978845608028654 * (x + 0.044715 * (x * x * x))))


def _rms(x):
    return lax.rsqrt(jnp.mean(x * x, axis=-1, keepdims=True) + EPS)


def _dot(a, b):
    return jnp.dot(a, b, preferred_element_type=F32)


def _dot_nt(a, b):
    return lax.dot_general(a, b, (((1,), (1,)), ((), ())), preferred_element_type=F32)


def _ada_kernel(c_ref, w_ref, b_ref, o_ref):
    c = c_ref[...]
    cs = c * jax.nn.sigmoid(c)
    o_ref[...] = jnp.sum(w_ref[...] * cs, axis=0, keepdims=True) + b_ref[...]


def _ada_mod(c_col, w_ada, b_ada):
    depth, d, n = w_ada.shape
    tn = 1024
    return pl.pallas_call(
        _ada_kernel,
        grid=(depth, n // tn),
        in_specs=[pl.BlockSpec((d, 1), lambda l, j: (0, 0)),
                  pl.BlockSpec((None, d, tn), lambda l, j: (l, 0, j)),
                  pl.BlockSpec((None, 1, tn), lambda l, j: (l, 0, j))],
        out_specs=pl.BlockSpec((None, 1, tn), lambda l, j: (l, 0, j)),
        out_shape=jax.ShapeDtypeStruct((depth, 1, n), F32),
        compiler_params=_params("arbitrary", "arbitrary"),
    )(c_col, w_ada, b_ada.reshape(depth, 1, n))


def _inproj_kernel(x_ref, g_ref, sc_ref, sh_ref, w_ref, o_ref, h_ref):
    @pl.when(pl.program_id(1) == 0)
    def _():
        x = x_ref[...]
        h = (x * _rms(x)) * g_ref[...]
        h_ref[...] = (h * (1.0 + sc_ref[...]) + sh_ref[...]).astype(BF16)

    o_ref[...] = _dot(h_ref[...], w_ref[...])


def _inproj(x, g, sc, sh, w):
    s, d = x.shape
    n = w.shape[1]
    tm, tn = min(s, 1024), PROJ_TILE
    row = pl.BlockSpec((1, d), lambda i, j: (0, 0))
    return pl.pallas_call(
        _inproj_kernel,
        grid=(s // tm, n // tn),
        in_specs=[pl.BlockSpec((tm, d), lambda i, j: (i, 0)), row, row, row,
                  pl.BlockSpec((d, tn), lambda i, j: (0, j))],
        out_specs=pl.BlockSpec((tm, tn), lambda i, j: (i, j)),
        out_shape=jax.ShapeDtypeStruct((s, n), F32),
        scratch_shapes=[pltpu.VMEM((tm, d), BF16)],
        compiler_params=_params("arbitrary", "arbitrary"),
    )(x, g, sc, sh, w)


def _mlstm_kernel(qk_ref, v_ref, o_ref, sm_ref, cw_ref, cb_ref, gb_ref, ng_ref, y_ref,
                  tail_ref, c_ref, n_ref, m_ref):
    L, Dh = M_CHUNK, M_HEAD_DIM

    @pl.when(pl.program_id(0) == 0)
    def _():
        tail_ref[...] = jnp.zeros_like(tail_ref)
        c_ref[...] = jnp.zeros_like(c_ref)
        n_ref[...] = jnp.zeros_like(n_ref)
        m_ref[...] = jnp.zeros_like(m_ref)

    raw = qk_ref[...]
    tail = tail_ref[...]
    row = lax.broadcasted_iota(jnp.int32, raw.shape, 0)

    def shifted(d):
        r = pltpu.roll(raw, d, axis=0)
        for k in range(d):
            r = jnp.where(row == k, tail[8 - d + k:8 - d + k + 1, :], r)
        return r

    cw = cw_ref[...]
    pre = (cw[3:4] * raw + cw[2:3] * shifted(1) + cw[1:2] * shifted(2) + cw[0:1] * shifted(3)
           + cb_ref[...])
    tail_ref[...] = raw[L - 8:L, :]
    qk = pre * jax.nn.sigmoid(pre)

    sm = sm_ref[...] + gb_ref[...]
    lf = jnp.minimum(sm, 0.0) - jnp.log1p(jnp.exp(-jnp.abs(sm)))
    rowg = lax.broadcasted_iota(jnp.int32, lf.shape, 0)
    g = lf
    for s in (1, 2, 4, 8, 16, 32, 64):
        g = g + jnp.where(rowg >= s, pltpu.roll(g, s, axis=0), 0.0)
    smT = sm.T
    gT = g.T
    ri = lax.broadcasted_iota(jnp.int32, (L, L), 0)
    ci = lax.broadcasted_iota(jnp.int32, (L, L), 1)
    causal = ri >= ci

    H = range(M_HEADS)
    hs = [slice(h * Dh, (h + 1) * Dh) for h in H]
    q = [qk[:, hs[h]] for h in H]
    k = [qk[:, M_WIDTH + h * Dh:M_WIDTH + (h + 1) * Dh] * (Dh ** -0.5) for h in H]
    v = [v_ref[:, hs[h]] for h in H]
    i_c = [sm[:, SMALL_I + h:SMALL_I + h + 1] for h in H]
    g_c = [g[:, SMALL_F + h:SMALL_F + h + 1] for h in H]
    i_r = [smT[SMALL_I + h:SMALL_I + h + 1, :] for h in H]
    g_r = [gT[SMALL_F + h:SMALL_F + h + 1, :] for h in H]
    g_last = [g_r[h][:, L - 1:L] for h in H]
    m_prev = [m_ref[h:h + 1, 0:1] for h in H]
    c_prev = [c_ref[h] for h in H]
    n_prev = [n_ref[h:h + 1, :] for h in H]
    qb = [q[h].astype(BF16) for h in H]
    kb = [k[h].astype(BF16) for h in H]
    vb = [v[h].astype(BF16) for h in H]
    a_max = [jnp.max(g_last[h] - g_r[h] + i_r[h], axis=1, keepdims=True) for h in H]
    dlog = [jnp.where(causal, g_c[h] - g_r[h] + i_r[h], -jnp.inf) for h in H]
    d_max = [jnp.max(dlog[h], axis=1, keepdims=True) for h in H]
    qn = [jnp.sum(q[h] * n_prev[h], axis=1, keepdims=True) for h in H]
    s_qk = [_dot_nt(qb[h], kb[h]) for h in H]
    q_c = [_dot(qb[h], c_prev[h].astype(BF16)) for h in H]
    w_c = [jnp.exp(g_last[h] - g_c[h] + i_c[h] - a_max[h]) for h in H]
    c_loc = [_dot(k[h].T.astype(BF16), (w_c[h] * v[h]).astype(BF16)) for h in H]
    n_loc = [jnp.sum(w_c[h] * k[h], axis=0, keepdims=True) for h in H]
    m_new = [jnp.maximum(g_last[h] + m_prev[h], a_max[h]) for h in H]
    s_old = [jnp.exp(g_last[h] + m_prev[h] - m_new[h]) for h in H]
    s_new = [jnp.exp(a_max[h] - m_new[h]) for h in H]
    for h in H:
        c_ref[h] = s_old[h] * c_prev[h] + s_new[h] * c_loc[h]
        n_ref[h:h + 1, :] = s_old[h] * n_prev[h] + s_new[h] * n_loc[h]
        m_ref[h:h + 1, :] = jnp.broadcast_to(m_new[h], (1, Dh))
    m_inter = [g_c[h] + m_prev[h] for h in H]
    m_t = [jnp.maximum(m_inter[h], d_max[h]) for h in H]
    sw = [s_qk[h] * jnp.exp(dlog[h] - m_t[h]) for h in H]
    inter = [jnp.exp(m_inter[h] - m_t[h]) for h in H]
    num = [_dot(sw[h].astype(BF16), vb[h]) + inter[h] * q_c[h] for h in H]
    den = [jnp.sum(sw[h], axis=1, keepdims=True) + inter[h] * qn[h] for h in H]
    hh = [num[h] / jnp.maximum(jnp.abs(den[h]), jnp.exp(-m_t[h])) for h in H]
    rms = [_rms(hh[h]) for h in H]
    for h in H:
        hn = (hh[h] * rms[h]) * ng_ref[:, hs[h]]
        y_ref[:, hs[h]] = (jax.nn.sigmoid(o_ref[:, hs[h]]) * hn).astype(BF16)


def _mlstm(proj, conv_w, conv_b, gate_bias, norm_g):
    s = proj.shape[0]
    L = M_CHUNK
    full = lambda shape: pl.BlockSpec(shape, lambda c: (0,) * len(shape))
    return pl.pallas_call(
        _mlstm_kernel,
        grid=(s // L,),
        in_specs=[pl.BlockSpec((L, 2 * M_WIDTH), lambda c: (c, COL_MQK // (2 * M_WIDTH))),
                  pl.BlockSpec((L, M_WIDTH), lambda c: (c, COL_MV // M_WIDTH)),
                  pl.BlockSpec((L, M_WIDTH), lambda c: (c, COL_MO // M_WIDTH)),
                  pl.BlockSpec((L, 128), lambda c: (c, COL_SMALL // 128)),
                  full((M_CONV, 2 * M_WIDTH)), full((1, 2 * M_WIDTH)), full((1, 128)), full((1, M_WIDTH))],
        out_specs=pl.BlockSpec((L, M_WIDTH), lambda c: (c, 0)),
        out_shape=jax.ShapeDtypeStruct((s, M_WIDTH), BF16),
        scratch_shapes=[pltpu.VMEM((8, 2 * M_WIDTH), F32),
                        pltpu.VMEM((M_HEADS, M_HEAD_DIM, M_HEAD_DIM), F32),
                        pltpu.VMEM((8, M_HEAD_DIM), F32),
                        pltpu.VMEM((8, M_HEAD_DIM), F32)],
        compiler_params=_params("arbitrary"),
    )(proj, proj, proj, proj, conv_w, conv_b, gate_bias, norm_g)


def _split3(a):
    hi = a.astype(BF16)
    r = a - hi.astype(F32)
    mid = r.astype(BF16)
    lo = (r - mid.astype(F32)).astype(BF16)
    return hi, mid, lo


def _dot_f32(a, b):
    ah, am, al = _split3(a)
    bh, bm, bl = _split3(b)
    small = _dot(ah, bl) + _dot(am, bm) + _dot(al, bh)
    mid = _dot(ah, bm) + _dot(am, bh)
    return _dot(ah, bh) + (mid + small)


def _s5_prep_kernel(lrr_ref, lir_ref, lrc_ref, lic_ref, ldt_ref, btr_ref, bti_ref, ctr_ref, cti_ref, place_ref,
                    m_ref, pr_ref, pi_ref, rr_ref, ri_ref, ar_ref, ai_ref):
    T = S_T
    dt = jnp.exp(ldt_ref[...])
    lr, li = lrr_ref[...], lir_ref[...]
    mag = jnp.exp(lr * dt)
    a_re = mag * jnp.cos(li * dt)
    a_im = mag * jnp.sin(li * dt)
    den = lr * lr + li * li
    nr = a_re - 1.0
    z_re = (nr * lr + a_im * li) / den
    z_im = (a_im * lr - nr * li) / den
    btr, bti = btr_ref[...], bti_ref[...]
    bb_re = z_re * btr - z_im * bti
    bb_im = z_re * bti + z_im * btr

    def pow_row(e):
        mg = jnp.exp((lr * dt) * e)
        th = (li * dt) * e
        return mg * jnp.cos(th), mg * jnp.sin(th)

    jrow = lax.shift_right_logical(lax.broadcasted_iota(jnp.int32, btr.shape, 0), 4).astype(F32)
    p_re, p_im = pow_row(-jrow)
    left_re = bb_re * p_re - bb_im * p_im
    left_im = bb_re * p_im + bb_im * p_re
    q_re, q_im = pow_row((T - 1.0) - jrow)
    pr_ref[...] = bb_re * q_re - bb_im * q_im
    pi_ref[...] = bb_re * q_im + bb_im * q_re

    lrc, lic = lrc_ref[...], lic_ref[...]
    ctr, cti = ctr_ref[...], cti_ref[...]

    def pow_col(e):
        mg = jnp.exp((lrc * dt) * e)
        th = (lic * dt) * e
        return mg * jnp.cos(th), mg * jnp.sin(th)

    icol = lax.shift_right_logical(lax.broadcasted_iota(jnp.int32, ctr.shape, 1), 4).astype(F32)
    e_re, e_im = pow_col(icol)
    right_re = ctr * e_re - cti * e_im
    right_im = ctr * e_im + cti * e_re
    m = _dot_f32(left_re, right_re) - _dot_f32(left_im, right_im)
    mj = lax.shift_right_logical(lax.broadcasted_iota(jnp.int32, m.shape, 0), 4)
    mi = lax.shift_right_logical(lax.broadcasted_iota(jnp.int32, m.shape, 1), 4)
    m_ref[...] = _dot(jnp.where(mi >= mj, m, 0.0).astype(BF16), place_ref[...]).astype(BF16)
    f_re, f_im = pow_col(icol + 1.0)
    rr_ref[...] = ctr * f_re - cti * f_im
    ri_ref[...] = -(ctr * f_im + cti * f_re)
    mag_t = jnp.exp((lr * dt) * T)
    ar_ref[...] = mag_t * jnp.cos((li * dt) * T)
    ai_ref[...] = mag_t * jnp.sin((li * dt) * T)


def _s5_prep(lam_re, lam_im, log_dt, b_re, b_im, c_re, c_im):
    G, N, C, T = S_GROUPS, S_STATE, S_GROUP, S_T
    TC = T * C
    bt = lambda b: jnp.tile(jnp.swapaxes(b, 1, 2), (1, T, 1))
    ct = lambda c: jnp.tile(jnp.swapaxes(c, 1, 2), (1, 1, T))
    blk = lambda a, b: pl.BlockSpec((None, a, b), lambda g: (g, 0, 0))
    O, NO = S_OCT, G // S_OCT
    W = T * O * C
    col = np.arange(W)
    src = (col // (O * C)) * C + col % C
    place = np.stack([(np.arange(TC)[:, None] == src[None, :]) & ((col // C) % O == g)[None, :]
                      for g in range(O)]).astype(np.float32)
    outs = pl.pallas_call(
        _s5_prep_kernel,
        grid=(G,),
        in_specs=[blk(1, N), blk(1, N), blk(N, 1), blk(N, 1), blk(1, 1),
                  blk(TC, N), blk(TC, N), blk(N, TC), blk(N, TC),
                  pl.BlockSpec((None, TC, W), lambda g: (g % O, 0, 0))],
        out_specs=[blk(TC, W), blk(TC, N), blk(TC, N), blk(N, TC), blk(N, TC), blk(1, N), blk(1, N)],
        out_shape=[jax.ShapeDtypeStruct((G, TC, W), BF16),
                   jax.ShapeDtypeStruct((G, TC, N), F32), jax.ShapeDtypeStruct((G, TC, N), F32),
                   jax.ShapeDtypeStruct((G, N, TC), F32), jax.ShapeDtypeStruct((G, N, TC), F32),
                   jax.ShapeDtypeStruct((G, 1, N), F32), jax.ShapeDtypeStruct((G, 1, N), F32)],
        compiler_params=_params("arbitrary"),
    )(lam_re.reshape(G, 1, N), lam_im.reshape(G, 1, N), lam_re.reshape(G, N, 1), lam_im.reshape(G, N, 1),
      log_dt.reshape(G, 1, 1), bt(b_re), bt(b_im), ct(c_re), ct(c_im), jnp.asarray(place, BF16))
    m, pr, pi, rr, ri, ar, ai = outs

    m_oct = m.reshape(NO, O, T, C, W).transpose(0, 2, 1, 3, 4).reshape(NO, W, W)

    def p_oct(p):
        p5 = p.astype(BF16).reshape(NO, O, T, C, N)
        out = jnp.zeros((NO, T, O, C, O, N), BF16)
        for g in range(O):
            out = out.at[:, :, g, :, g, :].set(p5[:, g])
        return out.reshape(NO, W, O * N)

    def r_oct(r):
        r5 = r.astype(BF16).reshape(NO, O, N, T, C)
        out = jnp.zeros((NO, O, N, T, O, C), BF16)
        for g in range(O):
            out = out.at[:, g, :, :, g, :].set(r5[:, g])
        return out.reshape(NO, O * N, W)

    oct_row = lambda a: a.reshape(NO, 1, O * N)
    return (m_oct, p_oct(pr), p_oct(pi), r_oct(rr), r_oct(ri), oct_row(ar), oct_row(ai))


def _s5_kernel(u_ref, m_ref, pr_ref, pi_ref, rr_ref, ri_ref, ar_ref, ai_ref, y_ref,
               xr_s, xi_s, er_s, ei_s, st_s):
    T = S_T
    nch = u_ref.shape[0] // T

    @pl.when(pl.program_id(1) == 0)
    def _():
        st_s[...] = jnp.zeros_like(st_s)

    u = jnp.concatenate([u_ref[pl.ds(j, nch, stride=T), :] for j in range(T)], axis=1).astype(BF16)
    y_loc = _dot(u, m_ref[...])
    xr_s[...] = _dot(u, pr_ref[...])
    xi_s[...] = _dot(u, pi_ref[...])
    a_re, a_im = ar_ref[...], ai_ref[...]

    def body(b, carry):
        e_re, e_im = carry
        base = pl.multiple_of(b * 8, 8)
        xr = xr_s[pl.ds(base, 8), :]
        xi = xi_s[pl.ds(base, 8), :]
        for k in range(8):
            er_s[pl.ds(base + k, 1), :] = e_re
            ei_s[pl.ds(base + k, 1), :] = e_im
            n_re = a_re * e_re - a_im * e_im + xr[k:k + 1]
            n_im = a_re * e_im + a_im * e_re + xi[k:k + 1]
            e_re, e_im = n_re, n_im
        return e_re, e_im

    e_re, e_im = lax.fori_loop(0, nch // 8, body, (st_s[0:1, :], st_s[1:2, :]))
    st_s[0:1, :] = e_re
    st_s[1:2, :] = e_im
    y = (y_loc + _dot(er_s[...].astype(BF16), rr_ref[...]) + _dot(ei_s[...].astype(BF16), ri_ref[...]))
    lanes = y_ref.shape[1]
    for i in range(T):
        y_ref[pl.ds(i, nch, stride=T), :] = y[:, i * lanes:(i + 1) * lanes]


def _s5_scan(proj, ops):
    m, pr, pi, rr, ri, ar, ai = ops
    s = proj.shape[0]
    no, w, ns = pr.shape
    lanes = S_OCT * S_GROUP
    ts = min(s, 4096)
    nch = ts // S_T
    op = lambda a, b: pl.BlockSpec((None, a, b), lambda o, r: (o, 0, 0))
    return pl.pallas_call(
        _s5_kernel,
        grid=(no, s // ts),
        in_specs=[pl.BlockSpec((ts, lanes), lambda o, r: (r, COL_SU // lanes + o)),
                  op(w, w), op(w, ns), op(w, ns), op(ns, w), op(ns, w), op(1, ns), op(1, ns)],
        out_specs=pl.BlockSpec((None, ts, lanes), lambda o, r: (o, r, 0)),
        out_shape=jax.ShapeDtypeStruct((no, s, lanes), F32),
        scratch_shapes=[pltpu.VMEM((nch, ns), F32) for _ in range(4)] + [pltpu.VMEM((8, ns), F32)],
        compiler_params=_params("arbitrary", "arbitrary"),
    )(proj, m, pr, pi, rr, ri, ar, ai)


def _s5_glu_kernel(y_ref, u_ref, d_ref, w_ref, b_ref, o_ref):
    y = jnp.concatenate([y_ref[o] for o in range(y_ref.shape[0])], axis=1)
    y = _gelu(y + d_ref[...] * u_ref[...])
    z = _dot(y.astype(BF16), w_ref[...]) + b_ref[...]
    o_ref[...] = (y * jax.nn.sigmoid(z)).astype(BF16)


def _s5_glu(y, proj, d_skip, w_glu, b_glu):
    no, s, lanes = y.shape
    w = no * lanes
    tm = 1024
    row = pl.BlockSpec((1, w), lambda i: (0, 0))
    return pl.pallas_call(
        _s5_glu_kernel,
        grid=(s // tm,),
        in_specs=[pl.BlockSpec((no, tm, lanes), lambda i: (0, i, 0)),
                  pl.BlockSpec((tm, w), lambda i: (i, COL_SU // S_WIDTH)),
                  row, pl.BlockSpec((w, w), lambda i: (0, 0)), row],
        out_specs=pl.BlockSpec((tm, w), lambda i: (i, 0)),
        out_shape=jax.ShapeDtypeStruct((s, w), BF16),
        compiler_params=_params("arbitrary"),
    )(y, proj, d_skip, w_glu, b_glu)


def _s5(proj, ops, d_skip, w_glu, b_glu):
    return _s5_glu(_s5_scan(proj, ops), proj, d_skip, w_glu, b_glu)


def _rope_tab_kernel(pos_ref, inv_ref, sgn_ref, cos_ref, sin_ref):
    ang = pos_ref[...].astype(F32) * inv_ref[...]
    cos_ref[...] = jnp.cos(ang)
    sin_ref[...] = jnp.sin(ang) * sgn_ref[...]


def _rope_tables(pos_col):
    s = pos_col.shape[0]
    half = N_HEAD_DIM // 2
    inv = (ROPE_THETA ** (-jnp.arange(half, dtype=F32) / half))
    inv = jnp.concatenate([inv, inv]).reshape(1, N_HEAD_DIM)
    sgn = jnp.concatenate([-jnp.ones((half,), F32), jnp.ones((half,), F32)]).reshape(1, N_HEAD_DIM)
    tm = min(s, 2048)
    row = pl.BlockSpec((1, N_HEAD_DIM), lambda i: (0, 0))
    tab = pl.BlockSpec((tm, N_HEAD_DIM), lambda i: (i, 0))
    return pl.pallas_call(
        _rope_tab_kernel,
        grid=(s // tm,),
        in_specs=[pl.BlockSpec((tm, 1), lambda i: (i, 0)), row, row],
        out_specs=[tab, tab],
        out_shape=[jax.ShapeDtypeStruct((s, N_HEAD_DIM), F32)] * 2,
        compiler_params=_params("arbitrary"),
    )(pos_col, inv, sgn)


def _col_reduce(x, op):
    while x.shape[0] > 8 and x.shape[0] % 64 == 0:
        x = op(x.reshape(8, x.shape[0] // 8, x.shape[1]), axis=0)
    return op(x, axis=0, keepdims=True)


def _colmax(x):
    return _col_reduce(x, jnp.max)


def _colsum(x):
    return _col_reduce(x, jnp.sum)


def _rope(x, cos_f, sin_f):
    return x * cos_f + pltpu.roll(x, N_HEAD_DIM // 2, axis=1) * sin_f


def _nsa_pre_kernel(*refs):
    kv = refs[0:6 * N_KV_HEADS]
    cos_ref, sin_ref = refs[12], refs[13]
    kcmp_o, vcmp_o, kslc_o, vslct_o, kwin_o, vwint_o = refs[14:]
    cos_f, sin_f = cos_ref[...], sin_ref[...]
    tm = cos_f.shape[0]
    key = pl.program_id(0) * tm + lax.broadcasted_iota(jnp.int32, (tm, N_HEAD_DIM), 0)
    blk = lax.shift_right_logical(key, 6) & (SLC_TILE // SLC_BLOCK - 1)
    onehot = jnp.where(lax.broadcasted_iota(jnp.int32, (tm, N_HEAD_DIM), 1) == blk, 1.0, 0.0).astype(BF16)
    ones_rows = jnp.where(lax.broadcasted_iota(jnp.int32, (V_EXTRA, tm), 0) == 0, 1.0, 0.0)
    pieces = [pl.ds(r, tm // CMP_STRIDE, stride=CMP_STRIDE) for r in range(CMP_STRIDE)]
    for kh in range(N_KV_HEADS):
        kcmp_o[kh] = jnp.concatenate(
            [_rope(kv[0 * N_KV_HEADS + kh][rows, :], cos_ref[rows, :], sin_ref[rows, :]) for rows in pieces], axis=1)
        vcmp_o[kh] = jnp.concatenate([kv[1 * N_KV_HEADS + kh][rows, :] for rows in pieces], axis=1)
        kslc_o[kh] = jnp.concatenate(
            [_rope(kv[2 * N_KV_HEADS + kh][...], cos_f, sin_f).astype(BF16), onehot], axis=1)
        vslct_o[kh] = jnp.concatenate([kv[3 * N_KV_HEADS + kh][...].T, ones_rows], axis=0).astype(BF16)
        kwin_o[kh] = _rope(kv[4 * N_KV_HEADS + kh][...], cos_f, sin_f).astype(BF16)
        vwint_o[kh] = jnp.concatenate([kv[5 * N_KV_HEADS + kh][...].T, ones_rows], axis=0).astype(BF16)


def _nsa_pre(proj, cos_f, sin_f):
    s = proj.shape[0]
    Hk, Dh = N_KV_HEADS, N_HEAD_DIM
    tm = 512
    kv_specs = [pl.BlockSpec((tm, Dh), functools.partial(lambda i, c: (i, c), c=COL_NKV // Dh + n))
                for n in range(6 * Hk)]
    tab = pl.BlockSpec((tm, Dh), lambda i: (i, 0))
    std = pl.BlockSpec((Hk, tm, Dh), lambda i: (0, i, 0))
    tr = pl.BlockSpec((Hk, Dh + V_EXTRA, tm), lambda i: (0, 0, i))
    ext = pl.BlockSpec((Hk, tm, 2 * Dh), lambda i: (0, i, 0))
    pcs = pl.BlockSpec((Hk, tm // CMP_STRIDE, CMP_STRIDE * Dh), lambda i: (0, i, 0))
    return pl.pallas_call(
        _nsa_pre_kernel,
        grid=(s // tm,),
        in_specs=kv_specs + [tab, tab],
        out_specs=[pcs, pcs, ext, tr, std, tr],
        out_shape=[jax.ShapeDtypeStruct((Hk, s // CMP_STRIDE, CMP_STRIDE * Dh), F32),
                   jax.ShapeDtypeStruct((Hk, s // CMP_STRIDE, CMP_STRIDE * Dh), F32),
                   jax.ShapeDtypeStruct((Hk, s, 2 * Dh), BF16), jax.ShapeDtypeStruct((Hk, Dh + V_EXTRA, s), BF16),
                   jax.ShapeDtypeStruct((Hk, s, Dh), BF16), jax.ShapeDtypeStruct((Hk, Dh + V_EXTRA, s), BF16)],
        compiler_params=_params("arbitrary"),
    )(*([proj] * (6 * Hk)), cos_f, sin_f)


def _cmp_kernel(x_ref, p1_ref, p2_ref, wt_ref, wb_ref, w2_ref, o_ref):
    x = x_ref[...]
    first = _dot((x + p1_ref[...]).astype(BF16), wt_ref[...])
    second = _dot((x + p2_ref[...]).astype(BF16), wb_ref[...])
    npieces = x.shape[0]
    pre = first + pltpu.roll(second, npieces - 1, axis=0)
    o_ref[...] = _dot(_gelu(pre).astype(BF16), w2_ref[...])


def _compress(xp, pos, w1, w2):
    hk, npieces, half = xp.shape
    dh = half // CMP_STRIDE
    p1 = pos[:CMP_STRIDE].reshape(1, half)
    p2 = pos[CMP_STRIDE:].reshape(1, half)
    full = lambda shape: pl.BlockSpec(shape, lambda h: (0,) * len(shape))
    return pl.pallas_call(
        _cmp_kernel,
        grid=(hk,),
        in_specs=[pl.BlockSpec((None, npieces, half), lambda h: (h, 0, 0)),
                  full((1, half)), full((1, half)), full((half, dh)), full((half, dh)), full((dh, dh))],
        out_specs=pl.BlockSpec((None, npieces, dh), lambda h: (h, 0, 0)),
        out_shape=jax.ShapeDtypeStruct((hk, npieces, dh), F32),
        compiler_params=_params("arbitrary"),
    )(xp, p1, p2, w1[:half].astype(BF16), w1[half:].astype(BF16), w2.astype(BF16))


def _nsa_attn_kernel(*refs, seq):
    heads = [_nsa_head(kh, refs, seq) for kh in range(N_KV_HEADS)]
    n_tiles = heads[0][0]

    def run_tiles(base, count):
        for i in range(count):
            stages = [head[1 + (i & 1)](base + i) for head in heads]
            for stage in range(len(stages[0])):
                for head_stages in stages:
                    head_stages[stage]()

    def slc_pair(k2, carry):
        run_tiles(2 * k2, 2)
        return carry

    lax.fori_loop(0, (n_tiles + 1) // 2, slc_pair, 0)
    for head in heads:
        head[3]()


def _nsa_head(kh, refs, seq):
    NW = WINDOW // Q_BLOCK + 1
    q_ref, sm_ref, cos_ref, sin_ref, cend_ref = refs[:5]
    kc_ref, vct_ref, ks_ref, vst_ref, ksd_ref, vsd_ref = [r.at[kh] for r in refs[5:11]]
    kw_refs = [r.at[kh] for r in refs[11:11 + NW]]
    vw_refs = [r.at[kh] for r in refs[11 + NW:11 + 2 * NW]]
    y_ref, gate_s = refs[11 + 2 * NW:13 + 2 * NW]
    (bias_s, rhs_s, sa_s, sb_s, pa_s, pb_s, aa_s, ab_s, m_s, acc_s, ocmp_s, owin_s
     ) = [r.at[kh] for r in refs[13 + 2 * NW:]]
    Dh, QB, G, TK = N_HEAD_DIM, Q_BLOCK, N_REP, SLC_TILE
    n_slc = seq // SLC_BLOCK
    n_cmp = seq // CMP_STRIDE
    ratio = SLC_BLOCK // CMP_STRIDE
    qb = pl.program_id(0)
    t0 = qb * QB
    c2 = (Dh ** -0.5) * 1.4426950408889634
    cos_f, sin_f = cos_ref[...], sin_ref[...]
    q = q_ref[:, kh * G * Dh:(kh + 1) * G * Dh]
    q_t = jnp.concatenate(
        [(_rope(q[:, g * Dh:(g + 1) * Dh], cos_f, sin_f) * c2).T.astype(BF16) for g in range(G)],
        axis=1)
    if kh == 0:
        gate_s[...] = jax.nn.sigmoid(sm_ref[...]).T
    t_lane = t0 + lax.broadcasted_iota(jnp.int32, (1, QB), 1)

    s_all = _dot(kc_ref[...], q_t)
    valid = cend_ref[...] <= t0
    any_valid = t_lane >= CMP_BLOCK - 1
    pk = jnp.zeros((n_cmp, QB), F32)
    p_parts = []
    for g in range(G):
        s_g = jnp.where(valid, s_all[:, g * QB:(g + 1) * QB], NEG_INF)
        e = jnp.exp2(s_g - _colmax(s_g))
        p = e * jnp.where(any_valid, 1.0 / _colsum(e), 0.0)
        pk = pk + p
        p_parts.append(p.astype(BF16))
    o_cmp = _dot(vct_ref[...], jnp.concatenate(p_parts, axis=1))

    s = _dot(jnp.concatenate([r[...] for r in kw_refs], axis=0), q_t)
    wpos = t0 - WINDOW + lax.broadcasted_iota(jnp.int32, (WINDOW + QB, QB), 0)
    wbias = jnp.where(wpos <= t_lane, jnp.where(wpos > t_lane - WINDOW, jnp.where(wpos >= 0, 0.0, NEG_INF),
                                                NEG_INF), NEG_INF)
    s = s + jnp.concatenate([wbias] * G, axis=1)
    p = jnp.exp2(s - _colmax(s))
    o_win = _dot(jnp.concatenate([r[...] for r in vw_refs], axis=1), p.astype(BF16))
    o_win = o_win[0:Dh] / o_win[Dh:Dh + 1]

    rhs_s[0:Dh, :] = q_t
    rhs_s[Dh:2 * Dh, :] = jnp.zeros((Dh, G * QB), BF16)

    slabs = [pk[r * n_slc:(r + 1) * n_slc] for r in range(ratio)]
    row_a = lax.broadcasted_iota(jnp.int32, (n_slc, QB), 0)
    prev_last = jnp.where(row_a == 0, 0.0, pltpu.roll(slabs[ratio - 1], 1, axis=0))
    mid = slabs[0]
    for r in range(1, ratio - 1):
        mid = mid + slabs[r]
    imp = (prev_last + 2.0 * mid) + slabs[ratio - 1]
    tb = lax.shift_right_logical(t_lane, 6)
    allowed = row_a <= tb
    forced = (row_a == 0) | (row_a == tb) | (row_a == tb - 1)
    score = jnp.where(forced, -jnp.inf, jnp.where(allowed, imp, -jnp.inf))
    row_f = row_a.astype(F32)
    for _ in range(max(min(SLC_TOPK, n_slc) - 3, 0)):
        best = _colmax(score)
        idx = _col_reduce(jnp.where(score == best, row_f, float(n_slc)), jnp.min)
        score = jnp.where(row_f == idx, -jnp.inf, score)
    bias_s[...] = jnp.where(score == -jnp.inf, jnp.where(row_a < 2 * qb, 0.0, NEG_INF), NEG_INF)

    blocks_per_tile = TK // SLC_BLOCK
    n_tiles = (t0 + TK - 1) // TK

    def scores(kt, buf):
        live = kt < n_tiles
        kt = jnp.maximum(jnp.minimum(kt, n_tiles - 1), 0)
        b8 = bias_s[pl.ds(pl.multiple_of(kt * blocks_per_tile, blocks_per_tile), blocks_per_tile), :]
        b8 = jnp.where(live, b8, NEG_INF)
        rows = jnp.concatenate([jnp.concatenate([b8] * G, axis=1),
                                jnp.zeros((16 - blocks_per_tile, G * QB), F32)], axis=0)
        rhs_s[Dh:Dh + 16, :] = rows.astype(BF16)
        buf[...] = _dot(ks_ref[pl.ds(pl.multiple_of(kt * TK, TK), TK), :], rhs_s[...])

    def softmax(buf, p_buf, a_buf):
        s = buf[...]
        m_old = m_s[...]
        m_new = jnp.maximum(m_old, _colmax(s))
        a_buf[...] = jnp.exp2(m_old - m_new)
        p_buf[...] = jnp.exp2(s - m_new).astype(BF16)
        m_s[...] = m_new

    def pv_acc(kt, p_buf, a_buf):
        kt = jnp.maximum(jnp.minimum(kt, n_tiles - 1), 0)
        acc_s[...] = a_buf[...] * acc_s[...] + _dot(vst_ref[:, pl.ds(pl.multiple_of(kt * TK, TK), TK)], p_buf[...])

    s = _dot(ksd_ref[:, 0:Dh], q_t)
    kpos = t0 + lax.broadcasted_iota(jnp.int32, (QB, QB), 0)
    s = s + jnp.concatenate([jnp.where(kpos <= t_lane, 0.0, NEG_INF)] * G, axis=1)
    m0 = _colmax(s)
    p = jnp.exp2(s - m0)
    m_s[...] = m0
    acc_s[...] = _dot(vsd_ref[...], p.astype(BF16))

    scores(0, sa_s)
    scores(1, sb_s)
    softmax(sa_s, pa_s, aa_s)

    ocmp_s[...] = o_cmp
    owin_s[...] = o_win

    def finish():
        o_slc = acc_s[0:Dh, :] / acc_s[Dh:Dh + 1, :]
        outs = []
        for g in range(G):
            base = SMALL_G + (kh * G + g) * 3
            sl = slice(g * QB, (g + 1) * QB)
            o = (gate_s[base:base + 1, :] * ocmp_s[:, sl] + gate_s[base + 1:base + 2, :] * o_slc[:, sl]
                 + gate_s[base + 2:base + 3, :] * owin_s[:, sl])
            outs.append(o.T)
        y_ref[:, kh * G * Dh:(kh + 1) * G * Dh] = jnp.concatenate(outs, axis=1).astype(BF16)

    def half(kt, s_cur, p_cur, a_cur, s_nxt, p_nxt, a_nxt):
        return (lambda: scores(kt + 2, s_cur), lambda: softmax(s_nxt, p_nxt, a_nxt),
                lambda: pv_acc(kt, p_cur, a_cur))

    even = (sa_s, pa_s, aa_s)
    odd = (sb_s, pb_s, ab_s)
    return n_tiles, lambda kt: half(kt, *even, *odd), lambda kt: half(kt, *odd, *even), finish


def _nsa_attn(proj, cos_f, sin_f, kc, vct, ks, vst, kw, vwt):
    s = proj.shape[0]
    Hk, Dh, QB, G = N_KV_HEADS, N_HEAD_DIM, Q_BLOCK, N_REP
    n_cmp = s // CMP_STRIDE
    nw = WINDOW // QB + 1
    ratio = SLC_BLOCK // CMP_STRIDE
    row = np.arange(n_cmp)
    j_cmp = (row % (n_cmp // ratio)) * ratio + row // (n_cmp // ratio)
    cmp_end = jnp.asarray((j_cmp * CMP_STRIDE + (CMP_BLOCK - 1))[:, None] - np.arange(QB)[None, :], jnp.int32)
    whole = lambda a, b: pl.BlockSpec((Hk, a, b), lambda qb: (0, 0, 0), pipeline_mode=pl.Buffered(1))
    tab = pl.BlockSpec((QB, Dh), lambda qb: (qb, 0))
    kw_specs = [pl.BlockSpec((Hk, QB, Dh), functools.partial(
        lambda qb, w: (0, jnp.maximum(qb - (nw - 1) + w, 0), 0), w=w)) for w in range(nw)]
    vw_specs = [pl.BlockSpec((Hk, Dh + V_EXTRA, QB), functools.partial(
        lambda qb, w: (0, 0, jnp.maximum(qb - (nw - 1) + w, 0)), w=w)) for w in range(nw)]
    lanes = G * QB
    return pl.pallas_call(
        functools.partial(_nsa_attn_kernel, seq=s),
        grid=(s // QB,),
        in_specs=[pl.BlockSpec((QB, N_WIDTH), lambda qb: (qb, COL_NQ // N_WIDTH)),
                  pl.BlockSpec((QB, 128), lambda qb: (qb, COL_SMALL // 128)),
                  tab, tab, pl.BlockSpec((n_cmp, QB), lambda qb: (0, 0), pipeline_mode=pl.Buffered(1)),
                  whole(n_cmp, Dh), whole(Dh, n_cmp), whole(s, 2 * Dh), whole(Dh + V_EXTRA, s),
                  pl.BlockSpec((Hk, QB, 2 * Dh), lambda qb: (0, qb, 0)),
                  pl.BlockSpec((Hk, Dh + V_EXTRA, QB), lambda qb: (0, 0, qb))] + kw_specs + vw_specs,
        out_specs=pl.BlockSpec((QB, N_WIDTH), lambda qb: (qb, 0)),
        out_shape=jax.ShapeDtypeStruct((s, N_WIDTH), BF16),
        scratch_shapes=[pltpu.VMEM((128, QB), F32),
                        pltpu.VMEM((Hk, s // SLC_BLOCK, QB), F32),
                        pltpu.VMEM((Hk, 2 * Dh, lanes), BF16),
                        pltpu.VMEM((Hk, SLC_TILE, lanes), F32),
                        pltpu.VMEM((Hk, SLC_TILE, lanes), F32),
                        pltpu.VMEM((Hk, SLC_TILE, lanes), BF16),
                        pltpu.VMEM((Hk, SLC_TILE, lanes), BF16),
                        pltpu.VMEM((Hk, 1, lanes), F32),
                        pltpu.VMEM((Hk, 1, lanes), F32),
                        pltpu.VMEM((Hk, 1, lanes), F32),
                        pltpu.VMEM((Hk, Dh + V_EXTRA, lanes), F32),
                        pltpu.VMEM((Hk, Dh, lanes), F32),
                        pltpu.VMEM((Hk, Dh, lanes), F32)],
        compiler_params=_params("arbitrary"),
    )(proj, proj, cos_f, sin_f, cmp_end, kc, vct, ks, vst, ks, vst, *([kw] * nw), *([vwt] * nw))


def _slab_major(a):
    hk, n, dh = a.shape
    ratio = SLC_BLOCK // CMP_STRIDE
    return a.reshape(hk, n // ratio, ratio, dh).transpose(0, 2, 1, 3).reshape(hk, n, dh)


def _nsa(proj, cos_f, sin_f, cmp_pos, cmp_w1, cmp_w2):
    kcmp, vcmp, ks, vst, kw, vwt = _nsa_pre(proj, cos_f, sin_f)
    kc = _slab_major(_compress(kcmp, cmp_pos[0], cmp_w1[0], cmp_w2[0])).astype(BF16)
    vc = _slab_major(_compress(vcmp, cmp_pos[1], cmp_w1[1], cmp_w2[1])).astype(BF16)
    return _nsa_attn(proj, cos_f, sin_f, kc, jnp.swapaxes(vc, 1, 2), ks, vst, kw, vwt)


def _outproj_kernel(ym_ref, ys_ref, yn_ref, w_ref, x_ref, g_ref, gt_ref, o_ref):
    y = (_dot(ym_ref[...], w_ref[0:M_WIDTH, :]) + _dot(ys_ref[...], w_ref[M_WIDTH:M_WIDTH + S_WIDTH, :])
         + _dot(yn_ref[...], w_ref[M_WIDTH + S_WIDTH:, :]))
    o_ref[...] = x_ref[...] + gt_ref[...] * ((y * _rms(y)) * g_ref[...])


def _outproj(ym, ys, yn, w, x, g, gt):
    s, d = x.shape
    tm = 512
    row = pl.BlockSpec((1, d), lambda i: (0, 0))
    rows = lambda width: pl.BlockSpec((tm, width), lambda i: (i, 0))
    return pl.pallas_call(
        _outproj_kernel,
        grid=(s // tm,),
        in_specs=[rows(M_WIDTH), rows(S_WIDTH), rows(N_WIDTH), pl.BlockSpec((d, d), lambda i: (0, 0)),
                  rows(d), row, row],
        out_specs=rows(d),
        out_shape=jax.ShapeDtypeStruct((s, d), F32),
        compiler_params=_params("arbitrary"),
    )(ym, ys, yn, w, x, g, gt)


def _ffn_kernel(x_ref, g_ref, sc_ref, sh_ref, wa_ref, wu_ref, cw_ref, cb_ref, wd_ref, gp_ref, gt_ref,
                o_ref, h_ref, acc_ref, carry_ref):
    i = pl.program_id(0)
    j = pl.program_id(1)
    tm = x_ref.shape[0]

    @pl.when(j == 0)
    def _():
        x = x_ref[...]
        h = (x * _rms(x)) * g_ref[...]
        h_ref[...] = (h * (1.0 + sc_ref[...]) + sh_ref[...]).astype(BF16)
        acc_ref[...] = jnp.zeros_like(acc_ref)

    @pl.when(i == 0)
    def _():
        carry_ref[j] = jnp.zeros(carry_ref.shape[1:], F32)

    h = h_ref[...]
    a = _dot(h, wa_ref[...])
    u = _dot(h, wu_ref[...])
    prev = carry_ref[j]
    carry_ref[j] = a[tm - 8:tm, :]
    row = lax.broadcasted_iota(jnp.int32, a.shape, 0)
    a1 = jnp.where(row == 0, prev[7:8, :], pltpu.roll(a, 1, axis=0))
    a2 = jnp.where(row == 0, prev[6:7, :], jnp.where(row == 1, prev[7:8, :], pltpu.roll(a, 2, axis=0)))
    cw = cw_ref[...]
    conv = cw[0:1] * a2 + cw[1:2] * a1 + cw[2:3] * a + cb_ref[...]
    acc_ref[...] += _dot((_gelu(conv) * u).astype(BF16), wd_ref[...])

    @pl.when(j == pl.num_programs(1) - 1)
    def _():
        y = acc_ref[...]
        o_ref[...] = x_ref[...] + gt_ref[...] * ((y * _rms(y)) * gp_ref[...])


def _ffn(x, g, sc, sh, wa, wu, cw, cb, wd, gp, gt):
    s, d = x.shape
    fp = wa.shape[1]
    tm, tf = 512, FFN_TILE
    nj = fp // tf
    row = pl.BlockSpec((1, d), lambda i, j: (0, 0))
    return pl.pallas_call(
        _ffn_kernel,
        grid=(s // tm, nj),
        in_specs=[pl.BlockSpec((tm, d), lambda i, j: (i, 0)), row, row, row,
                  pl.BlockSpec((d, tf), lambda i, j: (0, j)),
                  pl.BlockSpec((d, tf), lambda i, j: (0, j)),
                  pl.BlockSpec((3, tf), lambda i, j: (0, j)),
                  pl.BlockSpec((1, tf), lambda i, j: (0, j)),
                  pl.BlockSpec((tf, d), lambda i, j: (j, 0)),
                  row, row],
        out_specs=pl.BlockSpec((tm, d), lambda i, j: (i, 0)),
        out_shape=jax.ShapeDtypeStruct((s, d), F32),
        scratch_shapes=[pltpu.VMEM((tm, d), BF16), pltpu.VMEM((tm, d), F32), pltpu.VMEM((nj, 8, tf), F32)],
        compiler_params=_params("arbitrary", "arbitrary"),
    )(x, g, sc, sh, wa, wu, cw, cb, wd, gp, gt)


def _proj_weight(w_in):
    d = w_in.shape[0]
    offs = np.cumsum([0, 2 * M_WIDTH, M_WIDTH, M_WIDTH, M_HEADS, M_HEADS, S_WIDTH, N_WIDTH,
                      6 * N_KV_HEADS * N_HEAD_DIM, 3 * N_HEADS])
    seg = [w_in[:, offs[k]:offs[k + 1]] for k in range(9)]
    m_qk, m_v, m_o, m_i, m_f, s_u, n_q, n_kv, n_g = seg
    small = jnp.concatenate([m_i, m_f, n_g], axis=1)
    pad = jnp.zeros((d, N_PROJ_PAD - COL_SMALL - small.shape[1]), w_in.dtype)
    return jnp.concatenate([m_qk, n_q, m_v, m_o, s_u, n_kv, small, pad], axis=1).astype(BF16)


def _pad_cols(a, n):
    return jnp.concatenate([a, jnp.zeros(a.shape[:-1] + (n - a.shape[-1],), a.dtype)], axis=-1)


def kernel(x, c, positions, w_ada, b_ada, g_pre_mix, g_post_mix, g_pre_ffn, g_post_ffn, w_in, w_out, m_conv_w, m_conv_b, m_i_bias, m_f_bias, m_norm_g, s_lam_re, s_lam_im, s_log_dt, s_b_re, s_b_im, s_c_re, s_c_im, s_d, s_w_glu, s_b_glu, n_cmp_pos, n_cmp_w1, n_cmp_w2, f_w_up, f_conv_w, f_conv_b, f_w_down):
    batch, seq, d = x.shape
    assert batch == 1 and d == D_MODEL and seq % 1024 == 0
    depth = w_ada.shape[0]
    xs = x.reshape(seq, d)
    mod = _ada_mod(c.reshape(d, 1), w_ada, b_ada)
    cos_f, sin_f = _rope_tables(positions.reshape(seq, 1))
    row = lambda a: a.reshape(1, -1)
    for l in range(depth):
        sh1, sc1, gt1, sh2, sc2, gt2 = [mod[l, :, k * d:(k + 1) * d] for k in range(6)]
        proj = _inproj(xs, row(g_pre_mix[l]), sc1, sh1, _proj_weight(w_in[l]))
        gate_bias = _pad_cols(jnp.concatenate([m_i_bias[l], m_f_bias[l]]).reshape(1, -1), 128)
        y_m = _mlstm(proj, m_conv_w[l], row(m_conv_b[l]), gate_bias, row(m_norm_g[l]))
        ops = _s5_prep(s_lam_re[l], s_lam_im[l], s_log_dt[l], s_b_re[l], s_b_im[l], s_c_re[l], s_c_im[l])
        y_s = _s5(proj, ops, row(s_d[l]), s_w_glu[l].astype(BF16), row(s_b_glu[l]))
        y_n = _nsa(proj, cos_f, sin_f, n_cmp_pos[l], n_cmp_w1[l], n_cmp_w2[l])
        xs = _outproj(y_m, y_s, y_n, w_out[l].astype(BF16), xs, row(g_post_mix[l]), gt1)
        wa = _pad_cols(f_w_up[l][:, :D_FF], D_FF_PAD).astype(BF16)
        wu = _pad_cols(f_w_up[l][:, D_FF:], D_FF_PAD).astype(BF16)
        wd = jnp.concatenate([f_w_down[l], jnp.zeros((D_FF_PAD - D_FF, d), F32)], axis=0).astype(BF16)
        xs = _ffn(xs, row(g_pre_ffn[l]), sc2, sh2, wa, wu, _pad_cols(f_conv_w[l], D_FF_PAD),
                  _pad_cols(row(f_conv_b[l]), D_FF_PAD), wd, row(g_post_ffn[l]), gt2)
    return xs.reshape(batch, seq, d)
```

```python
import functools

import numpy as np
import jax
import jax.numpy as jnp
from jax import lax
from jax.experimental import pallas as pl
from jax.experimental.pallas import tpu as pltpu

F32 = jnp.float32
BF16 = jnp.bfloat16

D_MODEL = 2048
EPS = 1e-6
M_WIDTH = 512
M_HEADS = 4
M_HEAD_DIM = 128
M_CHUNK = 128
M_CONV = 4
S_WIDTH = 512
S_GROUP = 16
S_GROUPS = 32
S_STATE = 64
S_T = 16
S_OCT = 8
N_WIDTH = 1024
N_HEAD_DIM = 128
N_HEADS = 8
N_KV_HEADS = 2
N_REP = N_HEADS // N_KV_HEADS
CMP_BLOCK = 32
CMP_STRIDE = 16
SLC_BLOCK = 64
SLC_TOPK = 16
WINDOW = 512
Q_BLOCK = 128
ROPE_THETA = 10000.0
NEG_INF = -1e30
FORCED_SCORE = 1e9
SLC_TILE = 512
V_EXTRA = 16
D_FF = 5504
D_FF_PAD = 5632
FFN_TILE = 512
COL_MQK, COL_NQ, COL_MV, COL_MO, COL_SU, COL_NKV, COL_SMALL = 0, 1024, 2048, 2560, 3072, 3584, 5120
N_PROJ_PAD = 5376
PROJ_TILE = 768
SMALL_I, SMALL_F, SMALL_G = 0, 4, 8

VMEM_LIMIT_BYTES = 58 * 1024 * 1024


def _params(*sem):
    return pltpu.CompilerParams(dimension_semantics=sem, vmem_limit_bytes=VMEM_LIMIT_BYTES)


def _gelu(x):
    return 0.5 * x * (1.0 + jnp.tanh(0.7978845608028654 * (x + 0.044715 * (x * x * x))))


def _rms(x):
    return lax.rsqrt(jnp.mean(x * x, axis=-1, keepdims=True) + EPS)


def _dot(a, b):
    return jnp.dot(a, b, preferred_element_type=F32)


def _dot_nt(a, b):
    return lax.dot_general(a, b, (((1,), (1,)), ((), ())), preferred_element_type=F32)


def _ada_kernel(c_ref, w_ref, b_ref, o_ref):
    c = c_ref[...]
    cs = c * jax.nn.sigmoid(c)
    o_ref[...] = jnp.sum(w_ref[...] * cs, axis=0, keepdims=True) + b_ref[...]


def _ada_mod(c_col, w_ada, b_ada):
    depth, d, n = w_ada.shape
    tn = 1024
    return pl.pallas_call(
        _ada_kernel,
        grid=(depth, n // tn),
        in_specs=[pl.BlockSpec((d, 1), lambda l, j: (0, 0)),
                  pl.BlockSpec((None, d, tn), lambda l, j: (l, 0, j)),
                  pl.BlockSpec((None, 1, tn), lambda l, j: (l, 0, j))],
        out_specs=pl.BlockSpec((None, 1, tn), lambda l, j: (l, 0, j)),
        out_shape=jax.ShapeDtypeStruct((depth, 1, n), F32),
        compiler_params=_params("arbitrary", "arbitrary"),
    )(c_col, w_ada, b_ada.reshape(depth, 1, n))


def _inproj_kernel(x_ref, g_ref, sc_ref, sh_ref, w_ref, o_ref, h_ref):
    @pl.when(pl.program_id(1) == 0)
    def _():
        x = x_ref[...]
        h = (x * _rms(x)) * g_ref[...]
        h_ref[...] = (h * (1.0 + sc_ref[...]) + sh_ref[...]).astype(BF16)

    o_ref[...] = _dot(h_ref[...], w_ref[...])


def _inproj(x, g, sc, sh, w):
    s, d = x.shape
    n = w.shape[1]
    tm, tn = min(s, 1024), PROJ_TILE
    row = pl.BlockSpec((1, d), lambda i, j: (0, 0))
    return pl.pallas_call(
        _inproj_kernel,
        grid=(s // tm, n // tn),
        in_specs=[pl.BlockSpec((tm, d), lambda i, j: (i, 0)), row, row, row,
                  pl.BlockSpec((d, tn), lambda i, j: (0, j))],
        out_specs=pl.BlockSpec((tm, tn), lambda i, j: (i, j)),
        out_shape=jax.ShapeDtypeStruct((s, n), F32),
        scratch_shapes=[pltpu.VMEM((tm, d), BF16)],
        compiler_params=_params("arbitrary", "arbitrary"),
    )(x, g, sc, sh, w)


def _mlstm_kernel(qk_ref, v_ref, o_ref, sm_ref, cw_ref, cb_ref, gb_ref, ng_ref, y_ref,
                  tail_ref, c_ref, n_ref, m_ref):
    L, Dh = M_CHUNK, M_HEAD_DIM

    @pl.when(pl.program_id(0) == 0)
    def _():
        tail_ref[...] = jnp.zeros_like(tail_ref)
        c_ref[...] = jnp.zeros_like(c_ref)
        n_ref[...] = jnp.zeros_like(n_ref)
        m_ref[...] = jnp.zeros_like(m_ref)

    raw = qk_ref[...]
    tail = tail_ref[...]
    row = lax.broadcasted_iota(jnp.int32, raw.shape, 0)

    def shifted(d):
        r = pltpu.roll(raw, d, axis=0)
        for k in range(d):
            r = jnp.where(row == k, tail[8 - d + k:8 - d + k + 1, :], r)
        return r

    cw = cw_ref[...]
    pre = (cw[3:4] * raw + cw[2:3] * shifted(1) + cw[1:2] * shifted(2) + cw[0:1] * shifted(3)
           + cb_ref[...])
    tail_ref[...] = raw[L - 8:L, :]
    qk = pre * jax.nn.sigmoid(pre)

    sm = sm_ref[...] + gb_ref[...]
    lf = jnp.minimum(sm, 0.0) - jnp.log1p(jnp.exp(-jnp.abs(sm)))
    rowg = lax.broadcasted_iota(jnp.int32, lf.shape, 0)
    g = lf
    for s in (1, 2, 4, 8, 16, 32, 64):
        g = g + jnp.where(rowg >= s, pltpu.roll(g, s, axis=0), 0.0)
    smT = sm.T
    gT = g.T
    ri = lax.broadcasted_iota(jnp.int32, (L, L), 0)
    ci = lax.broadcasted_iota(jnp.int32, (L, L), 1)
    causal = ri >= ci

    H = range(M_HEADS)
    hs = [slice(h * Dh, (h + 1) * Dh) for h in H]
    q = [qk[:, hs[h]] for h in H]
    k = [qk[:, M_WIDTH + h * Dh:M_WIDTH + (h + 1) * Dh] * (Dh ** -0.5) for h in H]
    v = [v_ref[:, hs[h]] for h in H]
    i_c = [sm[:, SMALL_I + h:SMALL_I + h + 1] for h in H]
    g_c = [g[:, SMALL_F + h:SMALL_F + h + 1] for h in H]
    i_r = [smT[SMALL_I + h:SMALL_I + h + 1, :] for h in H]
    g_r = [gT[SMALL_F + h:SMALL_F + h + 1, :] for h in H]
    g_last = [g_r[h][:, L - 1:L] for h in H]
    m_prev = [m_ref[h:h + 1, 0:1] for h in H]
    c_prev = [c_ref[h] for h in H]
    n_prev = [n_ref[h:h + 1, :] for h in H]
    qb = [q[h].astype(BF16) for h in H]
    kb = [k[h].astype(BF16) for h in H]
    vb = [v[h].astype(BF16) for h in H]
    a_max = [jnp.max(g_last[h] - g_r[h] + i_r[h], axis=1, keepdims=True) for h in H]
    dlog = [jnp.where(causal, g_c[h] - g_r[h] + i_r[h], -jnp.inf) for h in H]
    d_max = [jnp.max(dlog[h], axis=1, keepdims=True) for h in H]
    qn = [jnp.sum(q[h] * n_prev[h], axis=1, keepdims=True) for h in H]
    s_qk = [_dot_nt(qb[h], kb[h]) for h in H]
    q_c = [_dot(qb[h], c_prev[h].astype(BF16)) for h in H]
    w_c = [jnp.exp(g_last[h] - g_c[h] + i_c[h] - a_max[h]) for h in H]
    c_loc = [_dot(k[h].T.astype(BF16), (w_c[h] * v[h]).astype(BF16)) for h in H]
    n_loc = [jnp.sum(w_c[h] * k[h], axis=0, keepdims=True) for h in H]
    m_new = [jnp.maximum(g_last[h] + m_prev[h], a_max[h]) for h in H]
    s_old = [jnp.exp(g_last[h] + m_prev[h] - m_new[h]) for h in H]
    s_new = [jnp.exp(a_max[h] - m_new[h]) for h in H]
    for h in H:
        c_ref[h] = s_old[h] * c_prev[h] + s_new[h] * c_loc[h]
        n_ref[h:h + 1, :] = s_old[h] * n_prev[h] + s_new[h] * n_loc[h]
        m_ref[h:h + 1, :] = jnp.broadcast_to(m_new[h], (1, Dh))
    m_inter = [g_c[h] + m_prev[h] for h in H]
    m_t = [jnp.maximum(m_inter[h], d_max[h]) for h in H]
    sw = [s_qk[h] * jnp.exp(dlog[h] - m_t[h]) for h in H]
    inter = [jnp.exp(m_inter[h] - m_t[h]) for h in H]
    num = [_dot(sw[h].astype(BF16), vb[h]) + inter[h] * q_c[h] for h in H]
    den = [jnp.sum(sw[h], axis=1, keepdims=True) + inter[h] * qn[h] for h in H]
    hh = [num[h] / jnp.maximum(jnp.abs(den[h]), jnp.exp(-m_t[h])) for h in H]
    rms = [_rms(hh[h]) for h in H]
    for h in H:
        hn = (hh[h] * rms[h]) * ng_ref[:, hs[h]]
        y_ref[:, hs[h]] = (jax.nn.sigmoid(o_ref[:, hs[h]]) * hn).astype(BF16)


def _mlstm(proj, conv_w, conv_b, gate_bias, norm_g):
    s = proj.shape[0]
    L = M_CHUNK
    full = lambda shape: pl.BlockSpec(shape, lambda c: (0,) * len(shape))
    return pl.pallas_call(
        _mlstm_kernel,
        grid=(s // L,),
        in_specs=[pl.BlockSpec((L, 2 * M_WIDTH), lambda c: (c, COL_MQK // (2 * M_WIDTH))),
                  pl.BlockSpec((L, M_WIDTH), lambda c: (c, COL_MV // M_WIDTH)),
                  pl.BlockSpec((L, M_WIDTH), lambda c: (c, COL_MO // M_WIDTH)),
                  pl.BlockSpec((L, 128), lambda c: (c, COL_SMALL // 128)),
                  full((M_CONV, 2 * M_WIDTH)), full((1, 2 * M_WIDTH)), full((1, 128)), full((1, M_WIDTH))],
        out_specs=pl.BlockSpec((L, M_WIDTH), lambda c: (c, 0)),
        out_shape=jax.ShapeDtypeStruct((s, M_WIDTH), BF16),
        scratch_shapes=[pltpu.VMEM((8, 2 * M_WIDTH), F32),
                        pltpu.VMEM((M_HEADS, M_HEAD_DIM, M_HEAD_DIM), F32),
                        pltpu.VMEM((8, M_HEAD_DIM), F32),
                        pltpu.VMEM((8, M_HEAD_DIM), F32)],
        compiler_params=_params("arbitrary"),
    )(proj, proj, proj, proj, conv_w, conv_b, gate_bias, norm_g)


def _split3(a):
    hi = a.astype(BF16)
    r = a - hi.astype(F32)
    mid = r.astype(BF16)
    lo = (r - mid.astype(F32)).astype(BF16)
    return hi, mid, lo


def _dot_f32(a, b):
    ah, am, al = _split3(a)
    bh, bm, bl = _split3(b)
    small = _dot(ah, bl) + _dot(am, bm) + _dot(al, bh)
    mid = _dot(ah, bm) + _dot(am, bh)
    return _dot(ah, bh) + (mid + small)


def _s5_prep_kernel(lrr_ref, lir_ref, lrc_ref, lic_ref, ldt_ref, btr_ref, bti_ref, ctr_ref, cti_ref, place_ref,
                    m_ref, pr_ref, pi_ref, rr_ref, ri_ref, ar_ref, ai_ref):
    T = S_T
    dt = jnp.exp(ldt_ref[...])
    lr, li = lrr_ref[...], lir_ref[...]
    mag = jnp.exp(lr * dt)
    a_re = mag * jnp.cos(li * dt)
    a_im = mag * jnp.sin(li * dt)
    den = lr * lr + li * li
    nr = a_re - 1.0
    z_re = (nr * lr + a_im * li) / den
    z_im = (a_im * lr - nr * li) / den
    btr, bti = btr_ref[...], bti_ref[...]
    bb_re = z_re * btr - z_im * bti
    bb_im = z_re * bti + z_im * btr

    def pow_row(e):
        mg = jnp.exp((lr * dt) * e)
        th = (li * dt) * e
        return mg * jnp.cos(th), mg * jnp.sin(th)

    jrow = lax.shift_right_logical(lax.broadcasted_iota(jnp.int32, btr.shape, 0), 4).astype(F32)
    p_re, p_im = pow_row(-jrow)
    left_re = bb_re * p_re - bb_im * p_im
    left_im = bb_re * p_im + bb_im * p_re
    q_re, q_im = pow_row((T - 1.0) - jrow)
    pr_ref[...] = bb_re * q_re - bb_im * q_im
    pi_ref[...] = bb_re * q_im + bb_im * q_re

    lrc, lic = lrc_ref[...], lic_ref[...]
    ctr, cti = ctr_ref[...], cti_ref[...]

    def pow_col(e):
        mg = jnp.exp((lrc * dt) * e)
        th = (lic * dt) * e
        return mg * jnp.cos(th), mg * jnp.sin(th)

    icol = lax.shift_right_logical(lax.broadcasted_iota(jnp.int32, ctr.shape, 1), 4).astype(F32)
    e_re, e_im = pow_col(icol)
    right_re = ctr * e_re - cti * e_im
    right_im = ctr * e_im + cti * e_re
    m = _dot_f32(left_re, right_re) - _dot_f32(left_im, right_im)
    mj = lax.shift_right_logical(lax.broadcasted_iota(jnp.int32, m.shape, 0), 4)
    mi = lax.shift_right_logical(lax.broadcasted_iota(jnp.int32, m.shape, 1), 4)
    m_ref[...] = _dot(jnp.where(mi >= mj, m, 0.0).astype(BF16), place_ref[...]).astype(BF16)
    f_re, f_im = pow_col(icol + 1.0)
    rr_ref[...] = ctr * f_re - cti * f_im
    ri_ref[...] = -(ctr * f_im + cti * f_re)
    mag_t = jnp.exp((lr * dt) * T)
    ar_ref[...] = mag_t * jnp.cos((li * dt) * T)
    ai_ref[...] = mag_t * jnp.sin((li * dt) * T)


def _s5_prep(lam_re, lam_im, log_dt, b_re, b_im, c_re, c_im):
    G, N, C, T = S_GROUPS, S_STATE, S_GROUP, S_T
    TC = T * C
    bt = lambda b: jnp.tile(jnp.swapaxes(b, 1, 2), (1, T, 1))
    ct = lambda c: jnp.tile(jnp.swapaxes(c, 1, 2), (1, 1, T))
    blk = lambda a, b: pl.BlockSpec((None, a, b), lambda g: (g, 0, 0))
    O, NO = S_OCT, G // S_OCT
    W = T * O * C
    col = np.arange(W)
    src = (col // (O * C)) * C + col % C
    place = np.stack([(np.arange(TC)[:, None] == src[None, :]) & ((col // C) % O == g)[None, :]
                      for g in range(O)]).astype(np.float32)
    outs = pl.pallas_call(
        _s5_prep_kernel,
        grid=(G,),
        in_specs=[blk(1, N), blk(1, N), blk(N, 1), blk(N, 1), blk(1, 1),
                  blk(TC, N), blk(TC, N), blk(N, TC), blk(N, TC),
                  pl.BlockSpec((None, TC, W), lambda g: (g % O, 0, 0))],
        out_specs=[blk(TC, W), blk(TC, N), blk(TC, N), blk(N, TC), blk(N, TC), blk(1, N), blk(1, N)],
        out_shape=[jax.ShapeDtypeStruct((G, TC, W), BF16),
                   jax.ShapeDtypeStruct((G, TC, N), F32), jax.ShapeDtypeStruct((G, TC, N), F32),
                   jax.ShapeDtypeStruct((G, N, TC), F32), jax.ShapeDtypeStruct((G, N, TC), F32),
                   jax.ShapeDtypeStruct((G, 1, N), F32), jax.ShapeDtypeStruct((G, 1, N), F32)],
        compiler_params=_params("arbitrary"),
    )(lam_re.reshape(G, 1, N), lam_im.reshape(G, 1, N), lam_re.reshape(G, N, 1), lam_im.reshape(G, N, 1),
      log_dt.reshape(G, 1, 1), bt(b_re), bt(b_im), ct(c_re), ct(c_im), jnp.asarray(place, BF16))
    m, pr, pi, rr, ri, ar, ai = outs

    m_oct = m.reshape(NO, O, T, C, W).transpose(0, 2, 1, 3, 4).reshape(NO, W, W)

    def p_oct(p):
        p5 = p.astype(BF16).reshape(NO, O, T, C, N)
        out = jnp.zeros((NO, T, O, C, O, N), BF16)
        for g in range(O):
            out = out.at[:, :, g, :, g, :].set(p5[:, g])
        return out.reshape(NO, W, O * N)

    def r_oct(r):
        r5 = r.astype(BF16).reshape(NO, O, N, T, C)
        out = jnp.zeros((NO, O, N, T, O, C), BF16)
        for g in range(O):
            out = out.at[:, g, :, :, g, :].set(r5[:, g])
        return out.reshape(NO, O * N, W)

    oct_row = lambda a: a.reshape(NO, 1, O * N)
    return (m_oct, p_oct(pr), p_oct(pi), r_oct(rr), r_oct(ri), oct_row(ar), oct_row(ai))


def _s5_kernel(u_ref, m_ref, pr_ref, pi_ref, rr_ref, ri_ref, ar_ref, ai_ref, y_ref,
               xr_s, xi_s, er_s, ei_s, st_s):
    T = S_T
    nch = u_ref.shape[0] // T

    @pl.when(pl.program_id(1) == 0)
    def _():
        st_s[...] = jnp.zeros_like(st_s)

    u = jnp.concatenate([u_ref[pl.ds(j, nch, stride=T), :] for j in range(T)], axis=1).astype(BF16)
    y_loc = _dot(u, m_ref[...])
    xr_s[...] = _dot(u, pr_ref[...])
    xi_s[...] = _dot(u, pi_ref[...])
    a_re, a_im = ar_ref[...], ai_ref[...]

    def body(b, carry):
        e_re, e_im = carry
        base = pl.multiple_of(b * 8, 8)
        xr = xr_s[pl.ds(base, 8), :]
        xi = xi_s[pl.ds(base, 8), :]
        for k in range(8):
            er_s[pl.ds(base + k, 1), :] = e_re
            ei_s[pl.ds(base + k, 1), :] = e_im
            n_re = a_re * e_re - a_im * e_im + xr[k:k + 1]
            n_im = a_re * e_im + a_im * e_re + xi[k:k + 1]
            e_re, e_im = n_re, n_im
        return e_re, e_im

    e_re, e_im = lax.fori_loop(0, nch // 8, body, (st_s[0:1, :], st_s[1:2, :]))
    st_s[0:1, :] = e_re
    st_s[1:2, :] = e_im
    y = (y_loc + _dot(er_s[...].astype(BF16), rr_ref[...]) + _dot(ei_s[...].astype(BF16), ri_ref[...]))
    lanes = y_ref.shape[1]
    for i in range(T):
        y_ref[pl.ds(i, nch, stride=T), :] = y[:, i * lanes:(i + 1) * lanes]


def _s5_scan(proj, ops):
    m, pr, pi, rr, ri, ar, ai = ops
    s = proj.shape[0]
    no, w, ns = pr.shape
    lanes = S_OCT * S_GROUP
    ts = min(s, 4096)
    nch = ts // S_T
    op = lambda a, b: pl.BlockSpec((None, a, b), lambda o, r: (o, 0, 0))
    return pl.pallas_call(
        _s5_kernel,
        grid=(no, s // ts),
        in_specs=[pl.BlockSpec((ts, lanes), lambda o, r: (r, COL_SU // lanes + o)),
                  op(w, w), op(w, ns), op(w, ns), op(ns, w), op(ns, w), op(1, ns), op(1, ns)],
        out_specs=pl.BlockSpec((None, ts, lanes), lambda o, r: (o, r, 0)),
        out_shape=jax.ShapeDtypeStruct((no, s, lanes), F32),
        scratch_shapes=[pltpu.VMEM((nch, ns), F32) for _ in range(4)] + [pltpu.VMEM((8, ns), F32)],
        compiler_params=_params("arbitrary", "arbitrary"),
    )(proj, m, pr, pi, rr, ri, ar, ai)


def _s5_glu_kernel(y_ref, u_ref, d_ref, w_ref, b_ref, o_ref):
    y = jnp.concatenate([y_ref[o] for o in range(y_ref.shape[0])], axis=1)
    y = _gelu(y + d_ref[...] * u_ref[...])
    z = _dot(y.astype(BF16), w_ref[...]) + b_ref[...]
    o_ref[...] = (y * jax.nn.sigmoid(z)).astype(BF16)


def _s5_glu(y, proj, d_skip, w_glu, b_glu):
    no, s, lanes = y.shape
    w = no * lanes
    tm = 1024
    row = pl.BlockSpec((1, w), lambda i: (0, 0))
    return pl.pallas_call(
        _s5_glu_kernel,
        grid=(s // tm,),
        in_specs=[pl.BlockSpec((no, tm, lanes), lambda i: (0, i, 0)),
                  pl.BlockSpec((tm, w), lambda i: (i, COL_SU // S_WIDTH)),
                  row, pl.BlockSpec((w, w), lambda i: (0, 0)), row],
        out_specs=pl.BlockSpec((tm, w), lambda i: (i, 0)),
        out_shape=jax.ShapeDtypeStruct((s, w), BF16),
        compiler_params=_params("arbitrary"),
    )(y, proj, d_skip, w_glu, b_glu)


def _s5(proj, ops, d_skip, w_glu, b_glu):
    return _s5_glu(_s5_scan(proj, ops), proj, d_skip, w_glu, b_glu)


def _rope_tab_kernel(pos_ref, inv_ref, sgn_ref, cos_ref, sin_ref):
    ang = pos_ref[...].astype(F32) * inv_ref[...]
    cos_ref[...] = jnp.cos(ang)
    sin_ref[...] = jnp.sin(ang) * sgn_ref[...]


def _rope_tables(pos_col):
    s = pos_col.shape[0]
    half = N_HEAD_DIM // 2
    inv = (ROPE_THETA ** (-jnp.arange(half, dtype=F32) / half))
    inv = jnp.concatenate([inv, inv]).reshape(1, N_HEAD_DIM)
    sgn = jnp.concatenate([-jnp.ones((half,), F32), jnp.ones((half,), F32)]).reshape(1, N_HEAD_DIM)
    tm = min(s, 2048)
    row = pl.BlockSpec((1, N_HEAD_DIM), lambda i: (0, 0))
    tab = pl.BlockSpec((tm, N_HEAD_DIM), lambda i: (i, 0))
    return pl.pallas_call(
        _rope_tab_kernel,
        grid=(s // tm,),
        in_specs=[pl.BlockSpec((tm, 1), lambda i: (i, 0)), row, row],
        out_specs=[tab, tab],
        out_shape=[jax.ShapeDtypeStruct((s, N_HEAD_DIM), F32)] * 2,
        compiler_params=_params("arbitrary"),
    )(pos_col, inv, sgn)


def _col_reduce(x, op):
    while x.shape[0] > 8 and x.shape[0] % 64 == 0:
        x = op(x.reshape(8, x.shape[0] // 8, x.shape[1]), axis=0)
    return op(x, axis=0, keepdims=True)


def _colmax(x):
    return _col_reduce(x, jnp.max)


def _colsum(x):
    return _col_reduce(x, jnp.sum)


def _rope(x, cos_f, sin_f):
    return x * cos_f + pltpu.roll(x, N_HEAD_DIM // 2, axis=1) * sin_f


def _nsa_pre_kernel(*refs):
    kv = refs[0:6 * N_KV_HEADS]
    cos_ref, sin_ref = refs[12], refs[13]
    kcmp_o, vcmp_o, kslc_o, vslct_o, kwin_o, vwint_o = refs[14:]
    cos_f, sin_f = cos_ref[...], sin_ref[...]
    tm = cos_f.shape[0]
    key = pl.program_id(0) * tm + lax.broadcasted_iota(jnp.int32, (tm, N_HEAD_DIM), 0)
    blk = lax.shift_right_logical(key, 6) & (SLC_TILE // SLC_BLOCK - 1)
    onehot = jnp.where(lax.broadcasted_iota(jnp.int32, (tm, N_HEAD_DIM), 1) == blk, 1.0, 0.0).astype(BF16)
    ones_rows = jnp.where(lax.broadcasted_iota(jnp.int32, (V_EXTRA, tm), 0) == 0, 1.0, 0.0)
    pieces = [pl.ds(r, tm // CMP_STRIDE, stride=CMP_STRIDE) for r in range(CMP_STRIDE)]
    for kh in range(N_KV_HEADS):
        kcmp_o[kh] = jnp.concatenate(
            [_rope(kv[0 * N_KV_HEADS + kh][rows, :], cos_ref[rows, :], sin_ref[rows, :]) for rows in pieces], axis=1)
        vcmp_o[kh] = jnp.concatenate([kv[1 * N_KV_HEADS + kh][rows, :] for rows in pieces], axis=1)
        kslc_o[kh] = jnp.concatenate(
            [_rope(kv[2 * N_KV_HEADS + kh][...], cos_f, sin_f).astype(BF16), onehot], axis=1)
        vslct_o[kh] = jnp.concatenate([kv[3 * N_KV_HEADS + kh][...].T, ones_rows], axis=0).astype(BF16)
        kwin_o[kh] = _rope(kv[4 * N_KV_HEADS + kh][...], cos_f, sin_f).astype(BF16)
        vwint_o[kh] = jnp.concatenate([kv[5 * N_KV_HEADS + kh][...].T, ones_rows], axis=0).astype(BF16)


def _nsa_pre(proj, cos_f, sin_f):
    s = proj.shape[0]
    Hk, Dh = N_KV_HEADS, N_HEAD_DIM
    tm = 512
    kv_specs = [pl.BlockSpec((tm, Dh), functools.partial(lambda i, c: (i, c), c=COL_NKV // Dh + n))
                for n in range(6 * Hk)]
    tab = pl.BlockSpec((tm, Dh), lambda i: (i, 0))
    std = pl.BlockSpec((Hk, tm, Dh), lambda i: (0, i, 0))
    tr = pl.BlockSpec((Hk, Dh + V_EXTRA, tm), lambda i: (0, 0, i))
    ext = pl.BlockSpec((Hk, tm, 2 * Dh), lambda i: (0, i, 0))
    pcs = pl.BlockSpec((Hk, tm // CMP_STRIDE, CMP_STRIDE * Dh), lambda i: (0, i, 0))
    return pl.pallas_call(
        _nsa_pre_kernel,
        grid=(s // tm,),
        in_specs=kv_specs + [tab, tab],
        out_specs=[pcs, pcs, ext, tr, std, tr],
        out_shape=[jax.ShapeDtypeStruct((Hk, s // CMP_STRIDE, CMP_STRIDE * Dh), F32),
                   jax.ShapeDtypeStruct((Hk, s // CMP_STRIDE, CMP_STRIDE * Dh), F32),
                   jax.ShapeDtypeStruct((Hk, s, 2 * Dh), BF16), jax.ShapeDtypeStruct((Hk, Dh + V_EXTRA, s), BF16),
                   jax.ShapeDtypeStruct((Hk, s, Dh), BF16), jax.ShapeDtypeStruct((Hk, Dh + V_EXTRA, s), BF16)],
        compiler_params=_params("arbitrary"),
    )(*([proj] * (6 * Hk)), cos_f, sin_f)


def _cmp_kernel(x_ref, p1_ref, p2_ref, wt_ref, wb_ref, w2_ref, o_ref):
    x = x_ref[...]
    first = _dot((x + p1_ref[...]).astype(BF16), wt_ref[...])
    second = _dot((x + p2_ref[...]).astype(BF16), wb_ref[...])
    npieces = x.shape[0]
    pre = first + pltpu.roll(second, npieces - 1, axis=0)
    o_ref[...] = _dot(_gelu(pre).astype(BF16), w2_ref[...])


def _compress(xp, pos, w1, w2):
    hk, npieces, half = xp.shape
    dh = half // CMP_STRIDE
    p1 = pos[:CMP_STRIDE].reshape(1, half)
    p2 = pos[CMP_STRIDE:].reshape(1, half)
    full = lambda shape: pl.BlockSpec(shape, lambda h: (0,) * len(shape))
    return pl.pallas_call(
        _cmp_kernel,
        grid=(hk,),
        in_specs=[pl.BlockSpec((None, npieces, half), lambda h: (h, 0, 0)),
                  full((1, half)), full((1, half)), full((half, dh)), full((half, dh)), full((dh, dh))],
        out_specs=pl.BlockSpec((None, npieces, dh), lambda h: (h, 0, 0)),
        out_shape=jax.ShapeDtypeStruct((hk, npieces, dh), F32),
        compiler_params=_params("arbitrary"),
    )(xp, p1, p2, w1[:half].astype(BF16), w1[half:].astype(BF16), w2.astype(BF16))


def _nsa_attn_kernel(*refs, seq):
    heads = [_nsa_head(kh, refs, seq) for kh in range(N_KV_HEADS)]
    n_tiles = heads[0][0]

    def run_tiles(base, count):
        for i in range(count):
            stages = [head[1 + (i & 1)](base + i) for head in heads]
            for stage in range(len(stages[0])):
                for head_stages in stages:
                    head_stages[stage]()

    def slc_pair(k2, carry):
        run_tiles(2 * k2, 2)
        return carry

    lax.fori_loop(0, (n_tiles + 1) // 2, slc_pair, 0)
    for head in heads:
        head[3]()


def _nsa_head(kh, refs, seq):
    NW = WINDOW // Q_BLOCK + 1
    q_ref, sm_ref, cos_ref, sin_ref, cend_ref = refs[:5]
    kc_ref, vct_ref, ks_ref, vst_ref, ksd_ref, vsd_ref = [r.at[kh] for r in refs[5:11]]
    kw_refs = [r.at[kh] for r in refs[11:11 + NW]]
    vw_refs = [r.at[kh] for r in refs[11 + NW:11 + 2 * NW]]
    y_ref, gate_s = refs[11 + 2 * NW:13 + 2 * NW]
    (bias_s, rhs_s, sa_s, sb_s, pa_s, pb_s, aa_s, ab_s, m_s, acc_s, ocmp_s, owin_s
     ) = [r.at[kh] for r in refs[13 + 2 * NW:]]
    Dh, QB, G, TK = N_HEAD_DIM, Q_BLOCK, N_REP, SLC_TILE
    n_slc = seq // SLC_BLOCK
    n_cmp = seq // CMP_STRIDE
    ratio = SLC_BLOCK // CMP_STRIDE
    qb = pl.program_id(0)
    t0 = qb * QB
    c2 = (Dh ** -0.5) * 1.4426950408889634
    cos_f, sin_f = cos_ref[...], sin_ref[...]
    q = q_ref[:, kh * G * Dh:(kh + 1) * G * Dh]
    q_t = jnp.concatenate(
        [(_rope(q[:, g * Dh:(g + 1) * Dh], cos_f, sin_f) * c2).T.astype(BF16) for g in range(G)],
        axis=1)
    if kh == 0:
        gate_s[...] = jax.nn.sigmoid(sm_ref[...]).T
    t_lane = t0 + lax.broadcasted_iota(jnp.int32, (1, QB), 1)

    s_all = _dot(kc_ref[...], q_t)
    valid = cend_ref[...] <= t0
    any_valid = t_lane >= CMP_BLOCK - 1
    pk = jnp.zeros((n_cmp, QB), F32)
    p_parts = []
    for g in range(G):
        s_g = jnp.where(valid, s_all[:, g * QB:(g + 1) * QB], NEG_INF)
        e = jnp.exp2(s_g - _colmax(s_g))
        p = e * jnp.where(any_valid, 1.0 / _colsum(e), 0.0)
        pk = pk + p
        p_parts.append(p.astype(BF16))
    o_cmp = _dot(vct_ref[...], jnp.concatenate(p_parts, axis=1))

    s = _dot(jnp.concatenate([r[...] for r in kw_refs], axis=0), q_t)
    wpos = t0 - WINDOW + lax.broadcasted_iota(jnp.int32, (WINDOW + QB, QB), 0)
    wbias = jnp.where(wpos <= t_lane, jnp.where(wpos > t_lane - WINDOW, jnp.where(wpos >= 0, 0.0, NEG_INF),
                                                NEG_INF), NEG_INF)
    s = s + jnp.concatenate([wbias] * G, axis=1)
    p = jnp.exp2(s - _colmax(s))
    o_win = _dot(jnp.concatenate([r[...] for r in vw_refs], axis=1), p.astype(BF16))
    o_win = o_win[0:Dh] / o_win[Dh:Dh + 1]

    rhs_s[0:Dh, :] = q_t
    rhs_s[Dh:2 * Dh, :] = jnp.zeros((Dh, G * QB), BF16)

    slabs = [pk[r * n_slc:(r + 1) * n_slc] for r in range(ratio)]
    row_a = lax.broadcasted_iota(jnp.int32, (n_slc, QB), 0)
    prev_last = jnp.where(row_a == 0, 0.0, pltpu.roll(slabs[ratio - 1], 1, axis=0))
    mid = slabs[0]
    for r in range(1, ratio - 1):
        mid = mid + slabs[r]
    imp = (prev_last + 2.0 * mid) + slabs[ratio - 1]
    tb = lax.shift_right_logical(t_lane, 6)
    allowed = row_a <= tb
    forced = (row_a == 0) | (row_a == tb) | (row_a == tb - 1)
    score = jnp.where(forced, -jnp.inf, jnp.where(allowed, imp, -jnp.inf))
    row_f = row_a.astype(F32)
    for _ in range(max(min(SLC_TOPK, n_slc) - 3, 0)):
        best = _colmax(score)
        idx = _col_reduce(jnp.where(score == best, row_f, float(n_slc)), jnp.min)
        score = jnp.where(row_f == idx, -jnp.inf, score)
    bias_s[...] = jnp.where(score == -jnp.inf, jnp.where(row_a < 2 * qb, 0.0, NEG_INF), NEG_INF)

    blocks_per_tile = TK // SLC_BLOCK
    n_tiles = (t0 + TK - 1) // TK

    def scores(kt, buf):
        live = kt < n_tiles
        kt = jnp.maximum(jnp.minimum(kt, n_tiles - 1), 0)
        b8 = bias_s[pl.ds(pl.multiple_of(kt * blocks_per_tile, blocks_per_tile), blocks_per_tile), :]
        b8 = jnp.where(live, b8, NEG_INF)
        rows = jnp.concatenate([jnp.concatenate([b8] * G, axis=1),
                                jnp.zeros((16 - blocks_per_tile, G * QB), F32)], axis=0)
        rhs_s[Dh:Dh + 16, :] = rows.astype(BF16)
        buf[...] = _dot(ks_ref[pl.ds(pl.multiple_of(kt * TK, TK), TK), :], rhs_s[...])

    def softmax(buf, p_buf, a_buf):
        s = buf[...]
        m_old = m_s[...]
        m_new = jnp.maximum(m_old, _colmax(s))
        a_buf[...] = jnp.exp2(m_old - m_new)
        p_buf[...] = jnp.exp2(s - m_new).astype(BF16)
        m_s[...] = m_new

    def pv_acc(kt, p_buf, a_buf):
        kt = jnp.maximum(jnp.minimum(kt, n_tiles - 1), 0)
        acc_s[...] = a_buf[...] * acc_s[...] + _dot(vst_ref[:, pl.ds(pl.multiple_of(kt * TK, TK), TK)], p_buf[...])

    s = _dot(ksd_ref[:, 0:Dh], q_t)
    kpos = t0 + lax.broadcasted_iota(jnp.int32, (QB, QB), 0)
    s = s + jnp.concatenate([jnp.where(kpos <= t_lane, 0.0, NEG_INF)] * G, axis=1)
    m0 = _colmax(s)
    p = jnp.exp2(s - m0)
    m_s[...] = m0
    acc_s[...] = _dot(vsd_ref[...], p.astype(BF16))

    scores(0, sa_s)
    scores(1, sb_s)
    softmax(sa_s, pa_s, aa_s)

    ocmp_s[...] = o_cmp
    owin_s[...] = o_win

    def finish():
        o_slc = acc_s[0:Dh, :] / acc_s[Dh:Dh + 1, :]
        outs = []
        for g in range(G):
            base = SMALL_G + (kh * G + g) * 3
            sl = slice(g * QB, (g + 1) * QB)
            o = (gate_s[base:base + 1, :] * ocmp_s[:, sl] + gate_s[base + 1:base + 2, :] * o_slc[:, sl]
                 + gate_s[base + 2:base + 3, :] * owin_s[:, sl])
            outs.append(o.T)
        y_ref[:, kh * G * Dh:(kh + 1) * G * Dh] = jnp.concatenate(outs, axis=1).astype(BF16)

    def half(kt, s_cur, p_cur, a_cur, s_nxt, p_nxt, a_nxt):
        return (lambda: scores(kt + 2, s_cur), lambda: softmax(s_nxt, p_nxt, a_nxt),
                lambda: pv_acc(kt, p_cur, a_cur))

    even = (sa_s, pa_s, aa_s)
    odd = (sb_s, pb_s, ab_s)
    return n_tiles, lambda kt: half(kt, *even, *odd), lambda kt: half(kt, *odd, *even), finish


def _nsa_attn(proj, cos_f, sin_f, kc, vct, ks, vst, kw, vwt):
    s = proj.shape[0]
    Hk, Dh, QB, G = N_KV_HEADS, N_HEAD_DIM, Q_BLOCK, N_REP
    n_cmp = s // CMP_STRIDE
    nw = WINDOW // QB + 1
    ratio = SLC_BLOCK // CMP_STRIDE
    row = np.arange(n_cmp)
    j_cmp = (row % (n_cmp // ratio)) * ratio + row // (n_cmp // ratio)
    cmp_end = jnp.asarray((j_cmp * CMP_STRIDE + (CMP_BLOCK - 1))[:, None] - np.arange(QB)[None, :], jnp.int32)
    whole = lambda a, b: pl.BlockSpec((Hk, a, b), lambda qb: (0, 0, 0), pipeline_mode=pl.Buffered(1))
    tab = pl.BlockSpec((QB, Dh), lambda qb: (qb, 0))
    kw_specs = [pl.BlockSpec((Hk, QB, Dh), functools.partial(
        lambda qb, w: (0, jnp.maximum(qb - (nw - 1) + w, 0), 0), w=w)) for w in range(nw)]
    vw_specs = [pl.BlockSpec((Hk, Dh + V_EXTRA, QB), functools.partial(
        lambda qb, w: (0, 0, jnp.maximum(qb - (nw - 1) + w, 0)), w=w)) for w in range(nw)]
    lanes = G * QB
    return pl.pallas_call(
        functools.partial(_nsa_attn_kernel, seq=s),
        grid=(s // QB,),
        in_specs=[pl.BlockSpec((QB, N_WIDTH), lambda qb: (qb, COL_NQ // N_WIDTH)),
                  pl.BlockSpec((QB, 128), lambda qb: (qb, COL_SMALL // 128)),
                  tab, tab, pl.BlockSpec((n_cmp, QB), lambda qb: (0, 0), pipeline_mode=pl.Buffered(1)),
                  whole(n_cmp, Dh), whole(Dh, n_cmp), whole(s, 2 * Dh), whole(Dh + V_EXTRA, s),
                  pl.BlockSpec((Hk, QB, 2 * Dh), lambda qb: (0, qb, 0)),
                  pl.BlockSpec((Hk, Dh + V_EXTRA, QB), lambda qb: (0, 0, qb))] + kw_specs + vw_specs,
        out_specs=pl.BlockSpec((QB, N_WIDTH), lambda qb: (qb, 0)),
        out_shape=jax.ShapeDtypeStruct((s, N_WIDTH), BF16),
        scratch_shapes=[pltpu.VMEM((128, QB), F32),
                        pltpu.VMEM((Hk, s // SLC_BLOCK, QB), F32),
                        pltpu.VMEM((Hk, 2 * Dh, lanes), BF16),
                        pltpu.VMEM((Hk, SLC_TILE, lanes), F32),
                        pltpu.VMEM((Hk, SLC_TILE, lanes), F32),
                        pltpu.VMEM((Hk, SLC_TILE, lanes), BF16),
                        pltpu.VMEM((Hk, SLC_TILE, lanes), BF16),
                        pltpu.VMEM((Hk, 1, lanes), F32),
                        pltpu.VMEM((Hk, 1, lanes), F32),
                        pltpu.VMEM((Hk, 1, lanes), F32),
                        pltpu.VMEM((Hk, Dh + V_EXTRA, lanes), F32),
                        pltpu.VMEM((Hk, Dh, lanes), F32),
                        pltpu.VMEM((Hk, Dh, lanes), F32)],
        compiler_params=_params("arbitrary"),
    )(proj, proj, cos_f, sin_f, cmp_end, kc, vct, ks, vst, ks, vst, *([kw] * nw), *([vwt] * nw))


def _slab_major(a):
    hk, n, dh = a.shape
    ratio = SLC_BLOCK // CMP_STRIDE
    return a.reshape(hk, n // ratio, ratio, dh).transpose(0, 2, 1, 3).reshape(hk, n, dh)


def _nsa(proj, cos_f, sin_f, cmp_pos, cmp_w1, cmp_w2):
    kcmp, vcmp, ks, vst, kw, vwt = _nsa_pre(proj, cos_f, sin_f)
    kc = _slab_major(_compress(kcmp, cmp_pos[0], cmp_w1[0], cmp_w2[0])).astype(BF16)
    vc = _slab_major(_compress(vcmp, cmp_pos[1], cmp_w1[1], cmp_w2[1])).astype(BF16)
    return _nsa_attn(proj, cos_f, sin_f, kc, jnp.swapaxes(vc, 1, 2), ks, vst, kw, vwt)


def _outproj_kernel(ym_ref, y5_ref, u_ref, d_ref, wg_ref, bg_ref, yn_ref, w_ref, x_ref, g_ref, gt_ref, o_ref):
    y5 = jnp.concatenate([y5_ref[o] for o in range(y5_ref.shape[0])], axis=1)
    y5 = _gelu(y5 + d_ref[...] * u_ref[...])
    ys = (y5 * jax.nn.sigmoid(_dot(y5.astype(BF16), wg_ref[...]) + bg_ref[...])).astype(BF16)
    y = (_dot(ym_ref[...], w_ref[0:M_WIDTH, :]) + _dot(ys, w_ref[M_WIDTH:M_WIDTH + S_WIDTH, :])
         + _dot(yn_ref[...], w_ref[M_WIDTH + S_WIDTH:, :]))
    o_ref[...] = x_ref[...] + gt_ref[...] * ((y * _rms(y)) * g_ref[...])


def _outproj(ym, y5, proj, d_skip, w_glu, b_glu, yn, w, x, g, gt):
    s, d = x.shape
    no, _, lanes = y5.shape
    tm = 512
    row = pl.BlockSpec((1, d), lambda i: (0, 0))
    srow = pl.BlockSpec((1, S_WIDTH), lambda i: (0, 0))
    rows = lambda width: pl.BlockSpec((tm, width), lambda i: (i, 0))
    return pl.pallas_call(
        _outproj_kernel,
        grid=(s // tm,),
        in_specs=[rows(M_WIDTH), pl.BlockSpec((no, tm, lanes), lambda i: (0, i, 0)),
                  pl.BlockSpec((tm, S_WIDTH), lambda i: (i, COL_SU // S_WIDTH)), srow,
                  pl.BlockSpec((S_WIDTH, S_WIDTH), lambda i: (0, 0)), srow,
                  rows(N_WIDTH), pl.BlockSpec((d, d), lambda i: (0, 0)), rows(d), row, row],
        out_specs=rows(d),
        out_shape=jax.ShapeDtypeStruct((s, d), F32),
        compiler_params=_params("arbitrary"),
    )(ym, y5, proj, d_skip, w_glu, b_glu, yn, w, x, g, gt)


def _ffn_kernel(x_ref, g_ref, sc_ref, sh_ref, wa_ref, wu_ref, cw_ref, cb_ref, wd_ref, gp_ref, gt_ref,
                o_ref, h_ref, acc_ref, carry_ref):
    i = pl.program_id(0)
    j = pl.program_id(1)
    tm = x_ref.shape[0]

    @pl.when(j == 0)
    def _():
        x = x_ref[...]
        h = (x * _rms(x)) * g_ref[...]
        h_ref[...] = (h * (1.0 + sc_ref[...]) + sh_ref[...]).astype(BF16)
        acc_ref[...] = jnp.zeros_like(acc_ref)

    @pl.when(i == 0)
    def _():
        carry_ref[j] = jnp.zeros(carry_ref.shape[1:], F32)

    h = h_ref[...]
    a = _dot(h, wa_ref[...])
    u = _dot(h, wu_ref[...])
    prev = carry_ref[j]
    carry_ref[j] = a[tm - 8:tm, :]
    row = lax.broadcasted_iota(jnp.int32, a.shape, 0)
    a1 = jnp.where(row == 0, prev[7:8, :], pltpu.roll(a, 1, axis=0))
    a2 = jnp.where(row == 0, prev[6:7, :], jnp.where(row == 1, prev[7:8, :], pltpu.roll(a, 2, axis=0)))
    cw = cw_ref[...]
    conv = cw[0:1] * a2 + cw[1:2] * a1 + cw[2:3] * a + cb_ref[...]
    acc_ref[...] += _dot((_gelu(conv) * u).astype(BF16), wd_ref[...])

    @pl.when(j == pl.num_programs(1) - 1)
    def _():
        y = acc_ref[...]
        o_ref[...] = x_ref[...] + gt_ref[...] * ((y * _rms(y)) * gp_ref[...])


def _ffn(x, g, sc, sh, wa, wu, cw, cb, wd, gp, gt):
    s, d = x.shape
    fp = wa.shape[1]
    tm, tf = 512, FFN_TILE
    nj = fp // tf
    row = pl.BlockSpec((1, d), lambda i, j: (0, 0))
    return pl.pallas_call(
        _ffn_kernel,
        grid=(s // tm, nj),
        in_specs=[pl.BlockSpec((tm, d), lambda i, j: (i, 0)), row, row, row,
                  pl.BlockSpec((d, tf), lambda i, j: (0, j)),
                  pl.BlockSpec((d, tf), lambda i, j: (0, j)),
                  pl.BlockSpec((3, tf), lambda i, j: (0, j)),
                  pl.BlockSpec((1, tf), lambda i, j: (0, j)),
                  pl.BlockSpec((tf, d), lambda i, j: (j, 0)),
                  row, row],
        out_specs=pl.BlockSpec((tm, d), lambda i, j: (i, 0)),
        out_shape=jax.ShapeDtypeStruct((s, d), F32),
        scratch_shapes=[pltpu.VMEM((tm, d), BF16), pltpu.VMEM((tm, d), F32), pltpu.VMEM((nj, 8, tf), F32)],
        compiler_params=_params("arbitrary", "arbitrary"),
    )(x, g, sc, sh, wa, wu, cw, cb, wd, gp, gt)


def _proj_weight(w_in):
    d = w_in.shape[0]
    offs = np.cumsum([0, 2 * M_WIDTH, M_WIDTH, M_WIDTH, M_HEADS, M_HEADS, S_WIDTH, N_WIDTH,
                      6 * N_KV_HEADS * N_HEAD_DIM, 3 * N_HEADS])
    seg = [w_in[:, offs[k]:offs[k + 1]] for k in range(9)]
    m_qk, m_v, m_o, m_i, m_f, s_u, n_q, n_kv, n_g = seg
    small = jnp.concatenate([m_i, m_f, n_g], axis=1)
    pad = jnp.zeros((d, N_PROJ_PAD - COL_SMALL - small.shape[1]), w_in.dtype)
    return jnp.concatenate([m_qk, n_q, m_v, m_o, s_u, n_kv, small, pad], axis=1).astype(BF16)


def _pad_cols(a, n):
    return jnp.concatenate([a, jnp.zeros(a.shape[:-1] + (n - a.shape[-1],), a.dtype)], axis=-1)


def kernel(x, c, positions, w_ada, b_ada, g_pre_mix, g_post_mix, g_pre_ffn, g_post_ffn, w_in, w_out, m_conv_w, m_conv_b, m_i_bias, m_f_bias, m_norm_g, s_lam_re, s_lam_im, s_log_dt, s_b_re, s_b_im, s_c_re, s_c_im, s_d, s_w_glu, s_b_glu, n_cmp_pos, n_cmp_w1, n_cmp_w2, f_w_up, f_conv_w, f_conv_b, f_w_down):
    batch, seq, d = x.shape
    assert batch == 1 and d == D_MODEL and seq % 1024 == 0
    depth = w_ada.shape[0]
    xs = x.reshape(seq, d)
    mod = _ada_mod(c.reshape(d, 1), w_ada, b_ada)
    cos_f, sin_f = _rope_tables(positions.reshape(seq, 1))
    row = lambda a: a.reshape(1, -1)
    for l in range(depth):
        sh1, sc1, gt1, sh2, sc2, gt2 = [mod[l, :, k * d:(k + 1) * d] for k in range(6)]
        proj = _inproj(xs, row(g_pre_mix[l]), sc1, sh1, _proj_weight(w_in[l]))
        gate_bias = _pad_cols(jnp.concatenate([m_i_bias[l], m_f_bias[l]]).reshape(1, -1), 128)
        y_m = _mlstm(proj, m_conv_w[l], row(m_conv_b[l]), gate_bias, row(m_norm_g[l]))
        ops = _s5_prep(s_lam_re[l], s_lam_im[l], s_log_dt[l], s_b_re[l], s_b_im[l], s_c_re[l], s_c_im[l])
        y_5 = _s5_scan(proj, ops)
        y_n = _nsa(proj, cos_f, sin_f, n_cmp_pos[l], n_cmp_w1[l], n_cmp_w2[l])
        xs = _outproj(y_m, y_5, proj, row(s_d[l]), s_w_glu[l].astype(BF16), row(s_b_glu[l]), y_n,
                      w_out[l].astype(BF16), xs, row(g_post_mix[l]), gt1)
        wa = _pad_cols(f_w_up[l][:, :D_FF], D_FF_PAD).astype(BF16)
        wu = _pad_cols(f_w_up[l][:, D_FF:], D_FF_PAD).astype(BF16)
        wd = jnp.concatenate([f_w_down[l], jnp.zeros((D_FF_PAD - D_FF, d), F32)], axis=0).astype(BF16)
        xs = _ffn(xs, row(g_pre_ffn[l]), sc2, sh2, wa, wu, _pad_cols(f_conv_w[l], D_FF_PAD),
                  _pad_cols(row(f_conv_b[l]), D_FF_PAD), wd, row(g_post_ffn[l]), gt2)
    return xs.reshape(batch, seq, d)
```
